```python
import jax, jax.numpy as jnp
from jax import lax
import numpy as np

D_MODEL = 1024
BATCH = 8
SEQ = 4096
DEPTH = 2

EXPAND = 2
D_INNER = EXPAND * D_MODEL
D_POOL = D_INNER // 2
D_SB = D_INNER - D_POOL
POOL_WINDOWS = (2, 4, 8, 16)
N_POOL_GROUPS = len(POOL_WINDOWS)
POOL_GROUP = D_POOL // N_POOL_GROUPS
SB_HEAD_DIM = 64
SB_HEADS = D_SB // SB_HEAD_DIM
SB_BLOCK = 128
CONV_WIDTH = 3
D_CONV = D_INNER
D_IN_EVEN = D_POOL + 3 * D_SB + D_INNER
D_IN_ODD = 3 * D_CONV + D_INNER
N_EVEN = (DEPTH + 1) // 2
N_ODD = DEPTH // 2
EPS = 1e-6

kernel_name = "hybrid_pool_stickbreak_shortconv_adaln"


def rmsnorm(x, g):
    xf = x.astype(jnp.float32)
    y = xf * lax.rsqrt(jnp.mean(xf * xf, axis=-1, keepdims=True) + EPS)
    return (y * g.astype(jnp.float32)).astype(x.dtype)


def adaln_params(c, w, b):
    m = jax.nn.silu(c) @ w + b
    shift, scale, gate = jnp.split(m, 3, axis=-1)
    return shift[:, None, :], scale[:, None, :], gate[:, None, :]


def pool_mixer(u, w_grp, scale):
    bsz, s, _ = u.shape
    uf = u.astype(jnp.float32)
    cs = jnp.cumsum(uf, axis=1)
    pos = jnp.arange(s, dtype=jnp.float32)
    outs = []
    for gi, w in enumerate(POOL_WINDOWS):
        sl = slice(gi * POOL_GROUP, (gi + 1) * POOL_GROUP)
        cg = cs[..., sl]
        prev = jnp.pad(cg, ((0, 0), (w, 0), (0, 0)))[:, :s]
        count = jnp.minimum(pos + 1.0, float(w))[None, :, None]
        outs.append((cg - prev) / count - uf[..., sl])
    p = jnp.stack(outs, axis=2).astype(u.dtype)
    y = jnp.einsum('bsgc,gcd->bsgd', p, w_grp).reshape(bsz, s, D_POOL)
    return y * scale


def stick_breaking_attention(q, k, v):
    bsz, s, _, _ = q.shape
    qh, kh, vh = (jnp.swapaxes(t, 1, 2) for t in (q, k, v))
    inv_sqrt = 1.0 / np.sqrt(SB_HEAD_DIM).astype(np.float32)
    outs = []
    for blk in range(s // SB_BLOCK):
        q0 = blk * SB_BLOCK
        end = q0 + SB_BLOCK
        qb = qh[:, :, q0:end]
        kb = kh[:, :, :end]
        vb = vh[:, :, :end]
        z = jnp.einsum('bhqd,bhkd->bhqk', qb, kb).astype(jnp.float32) * inv_sqrt
        q_pos = jnp.arange(q0, end)[:, None]
        k_pos = jnp.arange(end)[None, :]
        causal = k_pos < q_pos
        log_fail = jnp.where(causal, -jax.nn.softplus(z), 0.0)
        after = lax.cumsum(log_fail, axis=3, reverse=True) - log_fail
        a = jnp.where(causal, jnp.exp(jax.nn.log_sigmoid(z) + after), 0.0)
        outs.append(jnp.einsum('bhqk,bhkd->bhqd', a.astype(vb.dtype), vb))
    o = jnp.concatenate(outs, axis=2)
    return jnp.swapaxes(o, 1, 2).reshape(bsz, s, SB_HEADS * SB_HEAD_DIM)


def even_mixer(h, w_in, pool_w, pool_scale, w_out):
    bsz, s, _ = h.shape
    proj = h @ w_in
    u_pool, q, k, v, gate = jnp.split(
        proj, [D_POOL, D_POOL + D_SB, D_POOL + 2 * D_SB, D_POOL + 3 * D_SB], axis=-1)
    y_pool = pool_mixer(u_pool, pool_w, pool_scale)
    shp = (bsz, s, SB_HEADS, SB_HEAD_DIM)
    y_sb = stick_breaking_attention(q.reshape(shp), k.reshape(shp), v.reshape(shp))
    y = jnp.concatenate([y_pool, y_sb], axis=-1) * jax.nn.silu(gate)
    return y @ w_out


def odd_mixer(h, w_in, conv_w, conv_b, w_out):
    s = h.shape[1]
    proj = h @ w_in
    gb, gc, u, gate = jnp.split(proj, [D_CONV, 2 * D_CONV, 3 * D_CONV], axis=-1)
    u = gc * u
    up = jnp.pad(u, ((0, 0), (CONV_WIDTH - 1, 0), (0, 0)))
    conv = conv_b + sum(up[:, j:j + s] * conv_w[j] for j in range(CONV_WIDTH))
    y = gb * conv * jax.nn.silu(gate)
    return y @ w_out


def setup_inputs(seed: int = 0) -> dict:
    key = jax.random.key(seed)
    ks = jax.random.split(key, 16)
    f32 = jnp.float32
    D = D_MODEL
    nrm = lambda k, shp, sc: jax.random.normal(k, shp, f32) * sc
    return {
        "x": nrm(ks[0], (BATCH, SEQ, D), 1.0),
        "c": nrm(ks[1], (BATCH, D), 1.0),
        "norm_g": 1.0 + nrm(ks[2], (DEPTH, D), 0.02),
        "ada_w": nrm(ks[3], (DEPTH, D, 3 * D), 0.1 * D ** -0.5),
        "ada_b": nrm(ks[4], (DEPTH, 3 * D), 0.01),
        "even_w_in": nrm(ks[5], (N_EVEN, D, D_IN_EVEN), D ** -0.5),
        "pool_w": nrm(ks[6], (N_EVEN, N_POOL_GROUPS, POOL_GROUP, POOL_GROUP), POOL_GROUP ** -0.5),
        "pool_scale": 1.0 + nrm(ks[7], (N_EVEN, D_POOL), 0.02),
        "even_w_out": nrm(ks[8], (N_EVEN, D_INNER, D), D_INNER ** -0.5),
        "odd_w_in": nrm(ks[9], (N_ODD, D, D_IN_ODD), D ** -0.5),
        "conv_w": nrm(ks[10], (N_ODD, CONV_WIDTH, D_CONV), CONV_WIDTH ** -0.5),
        "conv_b": nrm(ks[11], (N_ODD, D_CONV), 0.01),
        "odd_w_out": nrm(ks[12], (N_ODD, D_INNER, D), D_INNER ** -0.5),
        "final_g": 1.0 + nrm(ks[13], (D,), 0.02),
    }


def reference(x, c, norm_g, ada_w, ada_b, even_w_in, pool_w, pool_scale, even_w_out,
              odd_w_in, conv_w, conv_b, odd_w_out, final_g):
    for i in range(DEPTH):
        shift, scale, gate = adaln_params(c, ada_w[i], ada_b[i])
        h = rmsnorm(x, norm_g[i]) * (1.0 + scale) + shift
        j = i // 2
        if i % 2 == 0:
            y = even_mixer(h, even_w_in[j], pool_w[j], pool_scale[j], even_w_out[j])
        else:
            y = odd_mixer(h, odd_w_in[j], conv_w[j], conv_b[j], odd_w_out[j])
        x = x + ((1.0 + gate) * y).astype(x.dtype)
    return rmsnorm(x, final_g)
```

```python
import functools

import jax
import jax.numpy as jnp
from jax import lax
from jax.experimental import pallas as pl
from jax.experimental.pallas import tpu as pltpu

D_MODEL = 1024
D_INNER = 2 * D_MODEL
D_POOL = D_INNER // 2
D_SB = D_INNER - D_POOL
POOL_WINDOWS = (2, 4, 8, 16)
POOL_GROUP = D_POOL // len(POOL_WINDOWS)
MAX_WINDOW = max(POOL_WINDOWS)
HEAD_DIM = 64
CONV_WIDTH = 3
D_CONV = D_INNER
EPS = 1e-6

LANES = 128
SUBLANES = 8
VMEM_LIMIT_BYTES = 56 * 1024 * 1024

TM_IN0 = 512
TM_OUT0 = 512
TM_L1 = 256
T_ATT = 256
CONV_CHUNK = 512

F32 = jnp.float32
BF16 = jnp.bfloat16


def _silu(x):
    return x * (1.0 / (1.0 + jnp.exp(-x)))


def _dot(a, b):
    return jnp.dot(a, b, preferred_element_type=F32)


def _modulated_norm(x, g, mod):
    y = x * lax.rsqrt(jnp.mean(x * x, axis=-1, keepdims=True) + EPS) * g
    return y * (1.0 + mod[:, D_MODEL:2 * D_MODEL]) + mod[:, :D_MODEL]


def _adaln_kernel(c_ref, w_ref, b_ref, o_ref):
    o_ref[0] = _dot(_silu(c_ref[...]), w_ref[0]) + b_ref[0]


def _adaln(c, ada_w, ada_b):
    depth, d, d3 = ada_w.shape
    bsz = c.shape[0]
    tn = D_MODEL
    return pl.pallas_call(
        _adaln_kernel,
        grid=(depth, d3 // tn),
        in_specs=[
            pl.BlockSpec((bsz, d), lambda i, j: (0, 0)),
            pl.BlockSpec((1, d, tn), lambda i, j: (i, 0, j)),
            pl.BlockSpec((1, 1, tn), lambda i, j: (i, 0, j)),
        ],
        out_specs=pl.BlockSpec((1, bsz, tn), lambda i, j: (i, 0, j)),
        out_shape=jax.ShapeDtypeStruct((depth, bsz, d3), F32),
        name="adaln",
    )(c, ada_w, ada_b.reshape(depth, 1, d3))


def _in0_kernel(x_ref, mod_ref, g_ref, w_ref, pw_ref, ps_ref,
                yg_ref, q_ref, k_ref, v_ref, sg_ref, ext_ref):
    s = pl.program_id(1)
    tm = x_ref.shape[1]
    h = _modulated_norm(x_ref[0], g_ref[...], mod_ref[0]).astype(BF16)

    @pl.when(s == 0)
    def _():
        ext_ref[0:MAX_WINDOW, :] = jnp.zeros((MAX_WINDOW, D_POOL), F32)

    @pl.when(s > 0)
    def _():
        ext_ref[0:MAX_WINDOW, :] = ext_ref[tm:tm + MAX_WINDOW, :]

    ext_ref[MAX_WINDOW:, :] = _dot(h, w_ref[:, 0:D_POOL])

    pos = s * tm + lax.broadcasted_iota(jnp.int32, (tm, 1), 0)
    gate0 = D_POOL + 3 * D_SB
    for gi, win in enumerate(POOL_WINDOWS):
        c0 = gi * POOL_GROUP
        cols = slice(c0, c0 + POOL_GROUP)
        u = ext_ref[MAX_WINDOW:, cols]
        acc = u
        for j in range(1, win):
            acc = acc + ext_ref[MAX_WINDOW - j:MAX_WINDOW - j + tm, cols]
        inv_cnt = 1.0 / jnp.minimum(pos + 1, win).astype(F32)
        p = acc * inv_cnt - u
        y = _dot(p.astype(BF16), pw_ref[gi]) * ps_ref[:, cols]
        gate = _dot(h, w_ref[:, gate0 + c0:gate0 + c0 + POOL_GROUP])
        yg_ref[0, :, cols] = (y * _silu(gate)).astype(BF16)

    q_ref[0] = (_dot(h, w_ref[:, D_POOL:D_POOL + D_SB]) * 0.125).astype(BF16)
    k_ref[0] = _dot(h, w_ref[:, D_POOL + D_SB:D_POOL + 2 * D_SB]).astype(BF16)
    v_ref[0] = _dot(h, w_ref[:, D_POOL + 2 * D_SB:D_POOL + 3 * D_SB]).astype(BF16)
    sg_ref[0] = _silu(_dot(h, w_ref[:, gate0 + D_POOL:])).astype(BF16)


def _in0(x, mod, g, w_in, pool_w, pool_scale):
    bsz, seq, d = x.shape
    tm = TM_IN0
    n_in = w_in.shape[1]
    row_spec = pl.BlockSpec((1, tm, D_SB), lambda b, s: (b, s, 0))
    out_sds = jax.ShapeDtypeStruct((bsz, seq, D_SB), BF16)
    return pl.pallas_call(
        _in0_kernel,
        grid=(bsz, seq // tm),
        in_specs=[
            pl.BlockSpec((1, tm, d), lambda b, s: (b, s, 0)),
            pl.BlockSpec((1, 1, 3 * d), lambda b, s: (b, 0, 0)),
            pl.BlockSpec((1, d), lambda b, s: (0, 0)),
            pl.BlockSpec((d, n_in), lambda b, s: (0, 0), pipeline_mode=pl.Buffered(1)),
            pl.BlockSpec(pool_w.shape, lambda b, s: (0, 0, 0), pipeline_mode=pl.Buffered(1)),
            pl.BlockSpec((1, D_POOL), lambda b, s: (0, 0)),
        ],
        out_specs=[row_spec] * 5,
        out_shape=[out_sds] * 5,
        scratch_shapes=[pltpu.VMEM((tm + MAX_WINDOW, D_POOL), F32)],
        compiler_params=pltpu.CompilerParams(
            dimension_semantics=("arbitrary", "arbitrary"),
            vmem_limit_bytes=VMEM_LIMIT_BYTES),
        name="in0",
    )(x, mod, g, w_in, pool_w, pool_scale)


def _softplus(z):
    return jnp.maximum(z, 0.0) + jnp.log(1.0 + jnp.exp(-jnp.abs(z)))


def _attn_kernel(q_ref, k_ref, v_ref, sg_ref, o_ref, acc_ref):
    i = pl.program_id(2)
    t = q_ref.shape[1]
    q2 = q_ref[0]
    lane = lax.broadcasted_iota(jnp.int32, (1, LANES), 1)
    head_lanes = (lane < HEAD_DIM, lane >= HEAD_DIM)
    qm = tuple(jnp.where(m, q2, jnp.zeros_like(q2)) for m in head_lanes)
    row = lax.broadcasted_iota(jnp.int32, (t, t), 0)
    col = lax.broadcasted_iota(jnp.int32, (t, t), 1)
    tri = (row > col).astype(BF16)
    causal = col < row

    def tile(j, carries, diag):
        k2 = k_ref[0, pl.ds(pl.multiple_of(j * t, t), t), :]
        v2 = v_ref[0, pl.ds(pl.multiple_of(j * t, t), t), :]
        new = []
        for hd in range(2):
            z = lax.dot_general(qm[hd], k2, (((1,), (1,)), ((), ())), preferred_element_type=F32)
            sp = _softplus(z)
            log_fail = -sp
            if diag:
                log_fail = jnp.where(causal, log_fail, 0.0)
            hi = log_fail.astype(BF16)
            lo = (log_fail - hi.astype(F32)).astype(BF16)
            after = _dot(hi, tri) + _dot(lo, tri) + carries[hd]
            a = jnp.exp((z - sp) + after)
            if diag:
                a = jnp.where(causal, a, 0.0)
            pv = _dot(a.astype(BF16), v2)
            if diag:
                acc_ref[hd] = pv
            else:
                acc_ref[hd] += pv
            new.append(carries[hd] + jnp.sum(log_fail, axis=1, keepdims=True))
        return tuple(new)

    zero = jnp.zeros((t, 1), F32)
    carries = tile(i, (zero, zero), True)
    lax.fori_loop(0, i, lambda n, c: tile(i - 1 - n, c, False), carries)
    o = jnp.where(head_lanes[0], acc_ref[0], acc_ref[1])
    o_ref[0] = (o * sg_ref[0].astype(F32)).astype(BF16)


def _attention(q, k, v, sg):
    bsz, seq, d = q.shape
    t = T_ATT
    blk = pl.BlockSpec((1, t, LANES), lambda b, hp, i: (b, i, hp))
    full = pl.BlockSpec((1, seq, LANES), lambda b, hp, i: (b, 0, hp))
    return pl.pallas_call(
        _attn_kernel,
        grid=(bsz, d // LANES, seq // t),
        in_specs=[blk, full, full, blk],
        out_specs=blk,
        out_shape=jax.ShapeDtypeStruct((bsz, seq, d), BF16),
        scratch_shapes=[pltpu.VMEM((2, t, LANES), F32)],
        compiler_params=pltpu.CompilerParams(
            dimension_semantics=("arbitrary", "arbitrary", "arbitrary"),
            vmem_limit_bytes=VMEM_LIMIT_BYTES),
        name="attn",
    )(q, k, v, sg)


def _out0_kernel(yp_ref, ys_ref, x_ref, mod_ref, w_ref, o_ref):
    y = _dot(yp_ref[0], w_ref[0:D_POOL, :]) + _dot(ys_ref[0], w_ref[D_POOL:, :])
    o_ref[0] = x_ref[0] + (1.0 + mod_ref[0][:, 2 * D_MODEL:]) * y


def _out0(yp, ys, x, mod, w_out):
    bsz, seq, d = x.shape
    tm = TM_OUT0
    half = pl.BlockSpec((1, tm, D_POOL), lambda b, s: (b, s, 0))
    row = pl.BlockSpec((1, tm, d), lambda b, s: (b, s, 0))
    return pl.pallas_call(
        _out0_kernel,
        grid=(bsz, seq // tm),
        in_specs=[
            half, half, row,
            pl.BlockSpec((1, 1, 3 * d), lambda b, s: (b, 0, 0)),
            pl.BlockSpec(w_out.shape, lambda b, s: (0, 0), pipeline_mode=pl.Buffered(1)),
        ],
        out_specs=row,
        out_shape=jax.ShapeDtypeStruct(x.shape, F32),
        compiler_params=pltpu.CompilerParams(
            dimension_semantics=("arbitrary", "arbitrary"),
            vmem_limit_bytes=VMEM_LIMIT_BYTES),
        name="out0",
    )(yp, ys, x, mod, w_out)


def _layer1_kernel(x_ref, mod_ref, g_ref, w_in_ref, cw_ref, cb_ref, w_out_ref, fg_ref,
                   o_ref, ext_ref, y_ref):
    s = pl.program_id(1)
    tm = x_ref.shape[1]
    pad = SUBLANES
    x = x_ref[0]
    mod = mod_ref[0]
    h = _modulated_norm(x, g_ref[...], mod).astype(BF16)

    @pl.when(s == 0)
    def _():
        ext_ref[0:pad, :] = jnp.zeros((pad, D_CONV), F32)

    @pl.when(s > 0)
    def _():
        ext_ref[0:pad, :] = ext_ref[tm:tm + pad, :]

    for c0 in range(0, D_CONV, CONV_CHUNK):
        cols = slice(c0, c0 + CONV_CHUNK)

        def proj(k):
            return _dot(h, w_in_ref[:, k * D_CONV + c0:k * D_CONV + c0 + CONV_CHUNK])

        u = proj(1) * proj(2)
        ext_ref[pad:, cols] = u
        conv = cb_ref[:, cols] + cw_ref[2:3, cols] * u
        for j in range(CONV_WIDTH - 1):
            shift = CONV_WIDTH - 1 - j
            conv = conv + cw_ref[j:j + 1, cols] * ext_ref[pad - shift:pad - shift + tm, cols]
        y_ref[:, cols] = (proj(0) * conv * _silu(proj(3))).astype(BF16)

    y = _dot(y_ref[...], w_out_ref[...])
    x2 = x + (1.0 + mod[:, 2 * D_MODEL:]) * y
    o_ref[0] = x2 * lax.rsqrt(jnp.mean(x2 * x2, axis=-1, keepdims=True) + EPS) * fg_ref[...]


def _layer1(x, mod, g, w_in, conv_w, conv_b, w_out, final_g):
    bsz, seq, d = x.shape
    tm = TM_L1
    row = pl.BlockSpec((1, tm, d), lambda b, s: (b, s, 0))
    const2 = lambda b, s: (0, 0)
    return pl.pallas_call(
        _layer1_kernel,
        grid=(bsz, seq // tm),
        in_specs=[
            row,
            pl.BlockSpec((1, 1, 3 * d), lambda b, s: (b, 0, 0)),
            pl.BlockSpec((1, d), const2),
            pl.BlockSpec(w_in.shape, const2, pipeline_mode=pl.Buffered(1)),
            pl.BlockSpec(conv_w.shape, const2),
            pl.BlockSpec((1, D_CONV), const2),
            pl.BlockSpec(w_out.shape, const2, pipeline_mode=pl.Buffered(1)),
            pl.BlockSpec((1, d), const2),
        ],
        out_specs=row,
        out_shape=jax.ShapeDtypeStruct(x.shape, F32),
        scratch_shapes=[pltpu.VMEM((tm + SUBLANES, D_CONV), F32),
                        pltpu.VMEM((tm, D_CONV), BF16)],
        compiler_params=pltpu.CompilerParams(
            dimension_semantics=("arbitrary", "arbitrary"),
            vmem_limit_bytes=VMEM_LIMIT_BYTES),
        name="layer1",
    )(x, mod, g, w_in, conv_w, conv_b, w_out, final_g)


def kernel(x, c, norm_g, ada_w, ada_b, even_w_in, pool_w, pool_scale, even_w_out,
           odd_w_in, conv_w, conv_b, odd_w_out, final_g):
    bsz = x.shape[0]
    mods = _adaln(c, ada_w, ada_b)
    mod0 = mods[0].reshape(bsz, 1, 3 * D_MODEL)
    mod1 = mods[1].reshape(bsz, 1, 3 * D_MODEL)

    yg_pool, q, k, v, sg = _in0(
        x, mod0, norm_g[0:1], even_w_in[0].astype(BF16), pool_w[0].astype(BF16), pool_scale[0:1])
    yg_sb = _attention(q, k, v, sg)
    x1 = _out0(yg_pool, yg_sb, x, mod0, even_w_out[0].astype(BF16))
    return _layer1(x1, mod1, norm_g[1:2], odd_w_in[0].astype(BF16), conv_w[0], conv_b[0:1],
                   odd_w_out[0].astype(BF16), final_g.reshape(1, D_MODEL))
```

```python
import functools

import jax
import jax.numpy as jnp
from jax import lax
from jax.experimental import pallas as pl
from jax.experimental.pallas import tpu as pltpu

D_MODEL = 1024
D_INNER = 2 * D_MODEL
D_POOL = D_INNER // 2
D_SB = D_INNER - D_POOL
POOL_WINDOWS = (2, 4, 8, 16)
POOL_GROUP = D_POOL // len(POOL_WINDOWS)
MAX_WINDOW = max(POOL_WINDOWS)
HEAD_DIM = 64
CONV_WIDTH = 3
D_CONV = D_INNER
EPS = 1e-6

LANES = 128
SUBLANES = 8
VMEM_LIMIT_BYTES = 56 * 1024 * 1024

TM_IN0 = 512
TM_OUT0 = 512
TM_L1 = 256
T_ATT = 256
CONV_CHUNK = 512

F32 = jnp.float32
BF16 = jnp.bfloat16

MASKED_LOG = -1e30


def _silu(x):
    return x * (1.0 / (1.0 + jnp.exp(-x)))


def _dot(a, b):
    return jnp.dot(a, b, preferred_element_type=F32)


def _modulated_norm(x, g, mod):
    y = x * lax.rsqrt(jnp.mean(x * x, axis=-1, keepdims=True) + EPS) * g
    return y * (1.0 + mod[:, D_MODEL:2 * D_MODEL]) + mod[:, :D_MODEL]


def _adaln_kernel(c_ref, w_ref, b_ref, o_ref):
    o_ref[0] = _dot(_silu(c_ref[...]), w_ref[0]) + b_ref[0]


def _adaln(c, ada_w, ada_b):
    depth, d, d3 = ada_w.shape
    bsz = c.shape[0]
    tn = D_MODEL
    return pl.pallas_call(
        _adaln_kernel,
        grid=(depth, d3 // tn),
        in_specs=[
            pl.BlockSpec((bsz, d), lambda i, j: (0, 0)),
            pl.BlockSpec((1, d, tn), lambda i, j: (i, 0, j)),
            pl.BlockSpec((1, 1, tn), lambda i, j: (i, 0, j)),
        ],
        out_specs=pl.BlockSpec((1, bsz, tn), lambda i, j: (i, 0, j)),
        out_shape=jax.ShapeDtypeStruct((depth, bsz, d3), F32),
        name="adaln",
    )(c, ada_w, ada_b.reshape(depth, 1, d3))


def _in0_kernel(x_ref, mod_ref, g_ref, w_ref, pw_ref, ps_ref,
                yg_ref, q_ref, k_ref, v_ref, sg_ref, ext_ref):
    s = pl.program_id(1)
    tm = x_ref.shape[1]
    h = _modulated_norm(x_ref[0], g_ref[...], mod_ref[0]).astype(BF16)

    @pl.when(s == 0)
    def _():
        ext_ref[0:MAX_WINDOW, :] = jnp.zeros((MAX_WINDOW, D_POOL), F32)

    @pl.when(s > 0)
    def _():
        ext_ref[0:MAX_WINDOW, :] = ext_ref[tm:tm + MAX_WINDOW, :]

    ext_ref[MAX_WINDOW:, :] = _dot(h, w_ref[:, 0:D_POOL])

    pos = s * tm + lax.broadcasted_iota(jnp.int32, (tm, 1), 0)
    gate0 = D_POOL + 3 * D_SB
    for gi, win in enumerate(POOL_WINDOWS):
        c0 = gi * POOL_GROUP
        cols = slice(c0, c0 + POOL_GROUP)
        u = ext_ref[MAX_WINDOW:, cols]
        acc = u
        for j in range(1, win):
            acc = acc + ext_ref[MAX_WINDOW - j:MAX_WINDOW - j + tm, cols]
        inv_cnt = 1.0 / jnp.minimum(pos + 1, win).astype(F32)
        p = acc * inv_cnt - u
        y = _dot(p.astype(BF16), pw_ref[gi]) * ps_ref[:, cols]
        gate = _dot(h, w_ref[:, gate0 + c0:gate0 + c0 + POOL_GROUP])
        yg_ref[0, :, cols] = (y * _silu(gate)).astype(BF16)

    q_ref[0] = (_dot(h, w_ref[:, D_POOL:D_POOL + D_SB]) * 0.125).astype(BF16)
    k_ref[0] = _dot(h, w_ref[:, D_POOL + D_SB:D_POOL + 2 * D_SB]).astype(BF16)
    v_ref[0] = _dot(h, w_ref[:, D_POOL + 2 * D_SB:D_POOL + 3 * D_SB]).astype(BF16)
    sg_ref[0] = _silu(_dot(h, w_ref[:, gate0 + D_POOL:])).astype(BF16)


def _in0(x, mod, g, w_in, pool_w, pool_scale):
    bsz, seq, d = x.shape
    tm = TM_IN0
    n_in = w_in.shape[1]
    row_spec = pl.BlockSpec((1, tm, D_SB), lambda b, s: (b, s, 0))
    out_sds = jax.ShapeDtypeStruct((bsz, seq, D_SB), BF16)
    return pl.pallas_call(
        _in0_kernel,
        grid=(bsz, seq // tm),
        in_specs=[
            pl.BlockSpec((1, tm, d), lambda b, s: (b, s, 0)),
            pl.BlockSpec((1, 1, 3 * d), lambda b, s: (b, 0, 0)),
            pl.BlockSpec((1, d), lambda b, s: (0, 0)),
            pl.BlockSpec((d, n_in), lambda b, s: (0, 0), pipeline_mode=pl.Buffered(1)),
            pl.BlockSpec(pool_w.shape, lambda b, s: (0, 0, 0), pipeline_mode=pl.Buffered(1)),
            pl.BlockSpec((1, D_POOL), lambda b, s: (0, 0)),
        ],
        out_specs=[row_spec] * 5,
        out_shape=[out_sds] * 5,
        scratch_shapes=[pltpu.VMEM((tm + MAX_WINDOW, D_POOL), F32)],
        compiler_params=pltpu.CompilerParams(
            dimension_semantics=("arbitrary", "arbitrary"),
            vmem_limit_bytes=VMEM_LIMIT_BYTES),
        name="in0",
    )(x, mod, g, w_in, pool_w, pool_scale)


ATT_DEPTH = 5
KIND_PLAIN, KIND_DIAGONAL, KIND_DEAD = 0, 1, 2


def _softplus(z):
    return jnp.maximum(z, 0.0) + jnp.log(1.0 + jnp.exp(-jnp.abs(z)))


def _attn_items(n_blocks):
    items = [(i, j, KIND_DIAGONAL if i == j else KIND_PLAIN)
             for i in range(n_blocks) for j in range(i, -1, -1)]
    n_trips = len(items) + ATT_DEPTH - 1
    n_trips += n_trips % 2
    dead = (0, 0, KIND_DEAD)

    def shifted(stage):
        row = [dead] * stage + items
        return row + [dead] * (n_trips - len(row))

    rows = [
        [i for i, _, _ in shifted(0)], [j for _, j, _ in shifted(0)],
        [kind for _, _, kind in shifted(1)],
        [int(kind == KIND_PLAIN) for _, _, kind in shifted(3)],
        [i for i, _, _ in shifted(4)], [j for _, j, _ in shifted(4)],
    ]
    return jnp.asarray(rows, jnp.int32).reshape(-1), n_trips


def _attn_kernel(tbl_ref, q_ref, k_ref, v_ref, sg_ref, o_ref,
                 acc_ref, z_ref, hi_ref, lo_ref, ls_ref, sum_ref, after_ref, a_ref, carry_ref,
                 bias_ref, *, n_trips):
    t = T_ATT
    n_blocks = q_ref.shape[1] // t
    lane = lax.broadcasted_iota(jnp.int32, (1, LANES), 1)
    head_lanes = (lane < HEAD_DIM, lane >= HEAD_DIM)
    row = lax.broadcasted_iota(jnp.int32, (t, t), 0)
    col = lax.broadcasted_iota(jnp.int32, (t, t), 1)
    neg_tri = jnp.where(row > col, -1.0, 0.0).astype(BF16)
    causal = col < row
    bias_ref[KIND_PLAIN] = jnp.zeros((t, t), F32)
    bias_ref[KIND_DIAGONAL] = jnp.where(causal, 0.0, MASKED_LOG)
    bias_ref[KIND_DEAD] = jnp.full((t, t), MASKED_LOG, F32)
    for ref in (acc_ref, z_ref, hi_ref, lo_ref, sum_ref, after_ref, a_ref, carry_ref):
        ref[...] = jnp.zeros_like(ref)
    ls_ref[...] = jnp.full(ls_ref.shape, MASKED_LOG, F32)

    def rows_of(blk):
        return pl.ds(pl.multiple_of(blk * t, t), t)

    def trip(n, par):
        prev = 1 - par
        qi_0, kj_0 = tbl_ref[n], tbl_ref[n_trips + n]
        kind_1 = tbl_ref[2 * n_trips + n]
        keep_3 = tbl_ref[3 * n_trips + n].astype(F32)
        qi_4, kj_4 = tbl_ref[4 * n_trips + n], tbl_ref[5 * n_trips + n]

        v2 = v_ref[0, rows_of(kj_4), :]
        for hd in range(2):
            acc_ref[qi_4, hd] += _dot(a_ref[prev, hd], v2)
        for hd in range(2):
            after_ref[par, hd] = _dot(hi_ref[prev, hd], neg_tri) + _dot(lo_ref[prev, hd], neg_tri)
        q2 = q_ref[0, rows_of(qi_0), :]
        k2 = k_ref[0, rows_of(kj_0), :]
        for hd in range(2):
            qm = jnp.where(head_lanes[hd], q2, jnp.zeros_like(q2))
            z_ref[par, hd] = lax.dot_general(qm, k2, (((1,), (1,)), ((), ())),
                                             preferred_element_type=F32)

        for hd in range(2):
            carry = carry_ref[hd] * keep_3
            a_ref[par, hd] = jnp.exp(ls_ref[par, hd] + after_ref[prev, hd] + carry).astype(BF16)
            carry_ref[hd] = carry - sum_ref[par, hd]
        for hd in range(2):
            z = z_ref[prev, hd] + bias_ref[kind_1]
            sp = _softplus(z)
            ls_ref[par, hd] = z - sp
            hi = sp.astype(BF16)
            hi_ref[par, hd] = hi
            lo_ref[par, hd] = (sp - hi.astype(F32)).astype(BF16)
            sum_ref[par, hd] = jnp.sum(sp, axis=1, keepdims=True)

    def two_trips(m, _):
        trip(2 * m, 0)
        trip(2 * m + 1, 1)
        return 0

    lax.fori_loop(0, n_trips // 2, two_trips, 0)

    for i in range(n_blocks):
        rows = slice(i * t, (i + 1) * t)
        o = jnp.where(head_lanes[0], acc_ref[i, 0], acc_ref[i, 1])
        o_ref[0, rows, :] = (o * sg_ref[0, rows, :].astype(F32)).astype(BF16)


def _attention(q, k, v, sg):
    bsz, seq, d = q.shape
    t = T_ATT
    table, n_trips = _attn_items(seq // t)
    full = pl.BlockSpec((1, seq, LANES), lambda b, hp, tbl: (b, 0, hp))
    return pl.pallas_call(
        functools.partial(_attn_kernel, n_trips=n_trips),
        grid_spec=pltpu.PrefetchScalarGridSpec(
            num_scalar_prefetch=1,
            grid=(bsz, d // LANES),
            in_specs=[full, full, full, full],
            out_specs=full,
            scratch_shapes=[
                pltpu.VMEM((seq // t, 2, t, LANES), F32),
                pltpu.VMEM((2, 2, t, t), F32),
                pltpu.VMEM((2, 2, t, t), BF16),
                pltpu.VMEM((2, 2, t, t), BF16),
                pltpu.VMEM((2, 2, t, t), F32),
                pltpu.VMEM((2, 2, t, 1), F32),
                pltpu.VMEM((2, 2, t, t), F32),
                pltpu.VMEM((2, 2, t, t), BF16),
                pltpu.VMEM((2, t, 1), F32),
                pltpu.VMEM((3, t, t), F32),
            ]),
        out_shape=jax.ShapeDtypeStruct((bsz, seq, d), BF16),
        compiler_params=pltpu.CompilerParams(
            dimension_semantics=("arbitrary", "arbitrary"),
            vmem_limit_bytes=VMEM_LIMIT_BYTES),
        name="attn",
    )(table, q, k, v, sg)


def _out0_kernel(yp_ref, ys_ref, x_ref, mod_ref, w_ref, o_ref):
    y = _dot(yp_ref[0], w_ref[0:D_POOL, :]) + _dot(ys_ref[0], w_ref[D_POOL:, :])
    o_ref[0] = x_ref[0] + (1.0 + mod_ref[0][:, 2 * D_MODEL:]) * y


def _out0(yp, ys, x, mod, w_out):
    bsz, seq, d = x.shape
    tm = TM_OUT0
    half = pl.BlockSpec((1, tm, D_POOL), lambda b, s: (b, s, 0))
    row = pl.BlockSpec((1, tm, d), lambda b, s: (b, s, 0))
    return pl.pallas_call(
        _out0_kernel,
        grid=(bsz, seq // tm),
        in_specs=[
            half, half, row,
            pl.BlockSpec((1, 1, 3 * d), lambda b, s: (b, 0, 0)),
            pl.BlockSpec(w_out.shape, lambda b, s: (0, 0), pipeline_mode=pl.Buffered(1)),
        ],
        out_specs=row,
        out_shape=jax.ShapeDtypeStruct(x.shape, F32),
        compiler_params=pltpu.CompilerParams(
            dimension_semantics=("arbitrary", "arbitrary"),
            vmem_limit_bytes=VMEM_LIMIT_BYTES),
        name="out0",
    )(yp, ys, x, mod, w_out)


def _layer1_kernel(x_ref, mod_ref, g_ref, w_in_ref, cw_ref, cb_ref, w_out_ref, fg_ref,
                   o_ref, ext_ref, y_ref):
    s = pl.program_id(1)
    tm = x_ref.shape[1]
    pad = SUBLANES
    x = x_ref[0]
    mod = mod_ref[0]
    h = _modulated_norm(x, g_ref[...], mod).astype(BF16)

    @pl.when(s == 0)
    def _():
        ext_ref[0:pad, :] = jnp.zeros((pad, D_CONV), F32)

    @pl.when(s > 0)
    def _():
        ext_ref[0:pad, :] = ext_ref[tm:tm + pad, :]

    for c0 in range(0, D_CONV, CONV_CHUNK):
        cols = slice(c0, c0 + CONV_CHUNK)

        def proj(k):
            return _dot(h, w_in_ref[:, k * D_CONV + c0:k * D_CONV + c0 + CONV_CHUNK])

        u = proj(1) * proj(2)
        ext_ref[pad:, cols] = u
        conv = cb_ref[:, cols] + cw_ref[2:3, cols] * u
        for j in range(CONV_WIDTH - 1):
            shift = CONV_WIDTH - 1 - j
            conv = conv + cw_ref[j:j + 1, cols] * ext_ref[pad - shift:pad - shift + tm, cols]
        y_ref[:, cols] = (proj(0) * conv * _silu(proj(3))).astype(BF16)

    y = _dot(y_ref[...], w_out_ref[...])
    x2 = x + (1.0 + mod[:, 2 * D_MODEL:]) * y
    o_ref[0] = x2 * lax.rsqrt(jnp.mean(x2 * x2, axis=-1, keepdims=True) + EPS) * fg_ref[...]


def _layer1(x, mod, g, w_in, conv_w, conv_b, w_out, final_g):
    bsz, seq, d = x.shape
    tm = TM_L1
    row = pl.BlockSpec((1, tm, d), lambda b, s: (b, s, 0))
    const2 = lambda b, s: (0, 0)
    return pl.pallas_call(
        _layer1_kernel,
        grid=(bsz, seq // tm),
        in_specs=[
            row,
            pl.BlockSpec((1, 1, 3 * d), lambda b, s: (b, 0, 0)),
            pl.BlockSpec((1, d), const2),
            pl.BlockSpec(w_in.shape, const2, pipeline_mode=pl.Buffered(1)),
            pl.BlockSpec(conv_w.shape, const2),
            pl.BlockSpec((1, D_CONV), const2),
            pl.BlockSpec(w_out.shape, const2, pipeline_mode=pl.Buffered(1)),
            pl.BlockSpec((1, d), const2),
        ],
        out_specs=row,
        out_shape=jax.ShapeDtypeStruct(x.shape, F32),
        scratch_shapes=[pltpu.VMEM((tm + SUBLANES, D_CONV), F32),
                        pltpu.VMEM((tm, D_CONV), BF16)],
        compiler_params=pltpu.CompilerParams(
            dimension_semantics=("arbitrary", "arbitrary"),
            vmem_limit_bytes=VMEM_LIMIT_BYTES),
        name="layer1",
    )(x, mod, g, w_in, conv_w, conv_b, w_out, final_g)


def kernel(x, c, norm_g, ada_w, ada_b, even_w_in, pool_w, pool_scale, even_w_out,
           odd_w_in, conv_w, conv_b, odd_w_out, final_g):
    bsz = x.shape[0]
    mods = _adaln(c, ada_w, ada_b)
    mod0 = mods[0].reshape(bsz, 1, 3 * D_MODEL)
    mod1 = mods[1].reshape(bsz, 1, 3 * D_MODEL)

    yg_pool, q, k, v, sg = _in0(
        x, mod0, norm_g[0:1], even_w_in[0].astype(BF16), pool_w[0].astype(BF16), pool_scale[0:1])
    yg_sb = _attention(q, k, v, sg)
    x1 = _out0(yg_pool, yg_sb, x, mod0, even_w_out[0].astype(BF16))
    return _layer1(x1, mod1, norm_g[1:2], odd_w_in[0].astype(BF16), conv_w[0], conv_b[0:1],
                   odd_w_out[0].astype(BF16), final_g.reshape(1, D_MODEL))
```

```python
import functools

import jax
import jax.numpy as jnp
from jax import lax
from jax.experimental import pallas as pl
from jax.experimental.pallas import tpu as pltpu

D_MODEL = 1024
D_INNER = 2 * D_MODEL
D_POOL = D_INNER // 2
D_SB = D_INNER - D_POOL
POOL_WINDOWS = (2, 4, 8, 16)
POOL_GROUP = D_POOL // len(POOL_WINDOWS)
MAX_WINDOW = max(POOL_WINDOWS)
HEAD_DIM = 64
CONV_WIDTH = 3
D_CONV = D_INNER
EPS = 1e-6

LANES = 128
SUBLANES = 8
VMEM_LIMIT_BYTES = 56 * 1024 * 1024

TM_IN0 = 512
TM_OUT0 = 512
TM_L1 = 256
T_ATT = 256
ATT_HEADS = 2
CONV_CHUNK = 512

F32 = jnp.float32
BF16 = jnp.bfloat16

MASKED_LOG = -1e30


def _silu(x):
    return x * (1.0 / (1.0 + jnp.exp(-x)))


def _dot(a, b):
    return jnp.dot(a, b, preferred_element_type=F32)


def _modulated_norm(x, g, mod):
    y = x * lax.rsqrt(jnp.mean(x * x, axis=-1, keepdims=True) + EPS) * g
    return y * (1.0 + mod[:, D_MODEL:2 * D_MODEL]) + mod[:, :D_MODEL]


def _adaln_kernel(c_ref, w_ref, b_ref, o_ref):
    o_ref[0] = _dot(_silu(c_ref[...]), w_ref[0]) + b_ref[0]


def _adaln(c, ada_w, ada_b):
    depth, d, d3 = ada_w.shape
    bsz = c.shape[0]
    tn = D_MODEL
    return pl.pallas_call(
        _adaln_kernel,
        grid=(depth, d3 // tn),
        in_specs=[
            pl.BlockSpec((bsz, d), lambda i, j: (0, 0)),
            pl.BlockSpec((1, d, tn), lambda i, j: (i, 0, j)),
            pl.BlockSpec((1, 1, tn), lambda i, j: (i, 0, j)),
        ],
        out_specs=pl.BlockSpec((1, bsz, tn), lambda i, j: (i, 0, j)),
        out_shape=jax.ShapeDtypeStruct((depth, bsz, d3), F32),
        name="adaln",
    )(c, ada_w, ada_b.reshape(depth, 1, d3))


def _in0_kernel(x_ref, mod_ref, g_ref, w_ref, pw_ref, ps_ref,
                yg_ref, q_ref, k_ref, v_ref, sg_ref, ext_ref):
    s = pl.program_id(1)
    tm = x_ref.shape[1]
    h = _modulated_norm(x_ref[0], g_ref[...], mod_ref[0]).astype(BF16)

    @pl.when(s == 0)
    def _():
        ext_ref[0:MAX_WINDOW, :] = jnp.zeros((MAX_WINDOW, D_POOL), F32)

    @pl.when(s > 0)
    def _():
        ext_ref[0:MAX_WINDOW, :] = ext_ref[tm:tm + MAX_WINDOW, :]

    ext_ref[MAX_WINDOW:, :] = _dot(h, w_ref[:, 0:D_POOL])

    pos = s * tm + lax.broadcasted_iota(jnp.int32, (tm, 1), 0)
    gate0 = D_POOL + 3 * D_SB
    for gi, win in enumerate(POOL_WINDOWS):
        c0 = gi * POOL_GROUP
        cols = slice(c0, c0 + POOL_GROUP)
        u = ext_ref[MAX_WINDOW:, cols]
        acc = u
        for j in range(1, win):
            acc = acc + ext_ref[MAX_WINDOW - j:MAX_WINDOW - j + tm, cols]
        inv_cnt = 1.0 / jnp.minimum(pos + 1, win).astype(F32)
        p = acc * inv_cnt - u
        y = _dot(p.astype(BF16), pw_ref[gi]) * ps_ref[:, cols]
        gate = _dot(h, w_ref[:, gate0 + c0:gate0 + c0 + POOL_GROUP])
        yg_ref[0, :, cols] = (y * _silu(gate)).astype(BF16)

    q_ref[0] = (_dot(h, w_ref[:, D_POOL:D_POOL + D_SB]) * 0.125).astype(BF16)
    k_ref[0] = _dot(h, w_ref[:, D_POOL + D_SB:D_POOL + 2 * D_SB]).astype(BF16)
    v_ref[0] = _dot(h, w_ref[:, D_POOL + 2 * D_SB:D_POOL + 3 * D_SB]).astype(BF16)
    sg_ref[0] = _silu(_dot(h, w_ref[:, gate0 + D_POOL:])).astype(BF16)


def _in0(x, mod, g, w_in, pool_w, pool_scale):
    bsz, seq, d = x.shape
    tm = TM_IN0
    n_in = w_in.shape[1]
    row_spec = pl.BlockSpec((1, tm, D_SB), lambda b, s: (b, s, 0))
    out_sds = jax.ShapeDtypeStruct((bsz, seq, D_SB), BF16)
    return pl.pallas_call(
        _in0_kernel,
        grid=(bsz, seq // tm),
        in_specs=[
            pl.BlockSpec((1, tm, d), lambda b, s: (b, s, 0)),
            pl.BlockSpec((1, 1, 3 * d), lambda b, s: (b, 0, 0)),
            pl.BlockSpec((1, d), lambda b, s: (0, 0)),
            pl.BlockSpec((d, n_in), lambda b, s: (0, 0), pipeline_mode=pl.Buffered(1)),
            pl.BlockSpec(pool_w.shape, lambda b, s: (0, 0, 0), pipeline_mode=pl.Buffered(1)),
            pl.BlockSpec((1, D_POOL), lambda b, s: (0, 0)),
        ],
        out_specs=[row_spec] * 5,
        out_shape=[out_sds] * 5,
        scratch_shapes=[pltpu.VMEM((tm + MAX_WINDOW, D_POOL), F32)],
        compiler_params=pltpu.CompilerParams(
            dimension_semantics=("arbitrary", "arbitrary"),
            vmem_limit_bytes=VMEM_LIMIT_BYTES),
        name="in0",
    )(x, mod, g, w_in, pool_w, pool_scale)


ATT_DEPTH = 5
KIND_PLAIN, KIND_DIAGONAL, KIND_DEAD = 0, 1, 2


def _softplus(z):
    return jnp.maximum(z, 0.0) + jnp.log(1.0 + jnp.exp(-jnp.abs(z)))


def _attn_items(n_blocks):
    items = [(i, j, KIND_DIAGONAL if i == j else KIND_PLAIN)
             for i in range(n_blocks) for j in range(i, -1, -1)]
    n_trips = len(items) + ATT_DEPTH - 1
    n_trips += n_trips % 2
    dead = (0, 0, KIND_DEAD)

    def shifted(stage):
        row = [dead] * stage + items
        return row + [dead] * (n_trips - len(row))

    rows = [
        [i for i, _, _ in shifted(0)], [j for _, j, _ in shifted(0)],
        [kind for _, _, kind in shifted(1)],
        [i for i, _, _ in shifted(4)], [j for _, j, _ in shifted(4)],
    ]
    return jnp.asarray(rows, jnp.int32).reshape(-1), n_trips


def _attn_kernel(tbl_ref, q_ref, k_ref, v_ref, sg_ref, o_ref,
                 acc_ref, qm_ref, z_ref, sp_ref, ls_ref, before_ref, after_ref, a_ref, run_ref,
                 bias_ref, tri_ref, *, n_trips):
    t = T_ATT
    n_blocks = q_ref.shape[1] // t
    lane = lax.broadcasted_iota(jnp.int32, (1, LANES), 1)
    head_lanes = (lane < HEAD_DIM, lane >= HEAD_DIM)
    heads = range(ATT_HEADS)

    def slab(hd):
        return slice((hd // 2) * LANES, (hd // 2 + 1) * LANES)
    row = lax.broadcasted_iota(jnp.int32, (t, t), 0)
    col = lax.broadcasted_iota(jnp.int32, (t, t), 1)
    tri_ref[...] = jnp.where(row > col, -1.0, 0.0).astype(BF16)
    causal = col < row
    bias_ref[KIND_PLAIN] = jnp.zeros((t, t), F32)
    bias_ref[KIND_DIAGONAL] = jnp.where(causal, 0.0, MASKED_LOG)
    bias_ref[KIND_DEAD] = jnp.full((t, t), MASKED_LOG, F32)
    for ref in (acc_ref, z_ref, sp_ref, before_ref, after_ref, a_ref, run_ref):
        ref[...] = jnp.zeros_like(ref)
    ls_ref[...] = jnp.full(ls_ref.shape, MASKED_LOG, BF16)
    for hd in heads:
        q2 = q_ref[0, :, slab(hd)]
        qm_ref[hd] = jnp.where(head_lanes[hd % 2], q2, jnp.zeros_like(q2))

    def rows_of(blk):
        return pl.ds(pl.multiple_of(blk * t, t), t)

    def trip(n, par):
        prev = 1 - par
        qi_0, kj_0 = tbl_ref[n], tbl_ref[n_trips + n]
        kind_1 = tbl_ref[2 * n_trips + n]
        qi_4, kj_4 = tbl_ref[3 * n_trips + n], tbl_ref[4 * n_trips + n]

        for hd in heads:
            acc_ref[qi_4, hd] += _dot(a_ref[prev, hd], v_ref[0, rows_of(kj_4), slab(hd)])
        for hd in heads:
            after = _dot(sp_ref[prev, hd], tri_ref[...]) + before_ref[prev, hd]
            after_ref[par, hd] = after.astype(BF16)
        for hd in heads:
            z_ref[par, hd] = lax.dot_general(
                qm_ref[hd, rows_of(qi_0), :], k_ref[0, rows_of(kj_0), slab(hd)],
                (((1,), (1,)), ((), ())), preferred_element_type=F32)

        for hd in heads:
            a_ref[par, hd] = jnp.exp(ls_ref[par, hd] + after_ref[prev, hd])
        keep_1 = jnp.where(kind_1 == KIND_PLAIN, 1, 0)
        for hd in heads:
            z = z_ref[prev, hd] + bias_ref[kind_1]
            sp = _softplus(z)
            ls_ref[par, hd] = (z - sp).astype(BF16)
            sp_ref[par, hd] = sp.astype(BF16)
            run = run_ref[keep_1, hd]
            before_ref[par, hd] = run
            run_ref[1, hd] = run - jnp.sum(sp, axis=1, keepdims=True)

    def two_trips(m, _):
        trip(2 * m, 0)
        trip(2 * m + 1, 1)
        return 0

    lax.fori_loop(0, n_trips // 2, two_trips, 0)

    for i in range(n_blocks):
        rows = slice(i * t, (i + 1) * t)
        for hd in range(0, ATT_HEADS, 2):
            o = jnp.where(head_lanes[0], acc_ref[i, hd], acc_ref[i, hd + 1])
            o_ref[0, rows, slab(hd)] = (o * sg_ref[0, rows, slab(hd)].astype(F32)).astype(BF16)


def _attention(q, k, v, sg):
    bsz, seq, d = q.shape
    t = T_ATT
    nh = ATT_HEADS
    width = nh * HEAD_DIM
    table, n_trips = _attn_items(seq // t)
    full = pl.BlockSpec((1, seq, width), lambda b, hg, tbl: (b, 0, hg))
    return pl.pallas_call(
        functools.partial(_attn_kernel, n_trips=n_trips),
        grid_spec=pltpu.PrefetchScalarGridSpec(
            num_scalar_prefetch=1,
            grid=(bsz, d // width),
            in_specs=[full, full, full, full],
            out_specs=full,
            scratch_shapes=[
                pltpu.VMEM((seq // t, nh, t, LANES), F32),
                pltpu.VMEM((nh, seq, LANES), BF16),
                pltpu.VMEM((2, nh, t, t), F32),
                pltpu.VMEM((2, nh, t, t), BF16),
                pltpu.VMEM((2, nh, t, t), BF16),
                pltpu.VMEM((2, nh, t, 1), F32),
                pltpu.VMEM((2, nh, t, t), BF16),
                pltpu.VMEM((2, nh, t, t), BF16),
                pltpu.VMEM((2, nh, t, 1), F32),
                pltpu.VMEM((3, t, t), F32),
                pltpu.VMEM((t, t), BF16),
            ]),
        out_shape=jax.ShapeDtypeStruct((bsz, seq, d), BF16),
        compiler_params=pltpu.CompilerParams(
            dimension_semantics=("arbitrary", "arbitrary"),
            vmem_limit_bytes=VMEM_LIMIT_BYTES),
        name="attn",
    )(table, q, k, v, sg)


def _out0_kernel(yp_ref, ys_ref, x_ref, mod_ref, w_ref, o_ref):
    y = _dot(yp_ref[0], w_ref[0:D_POOL, :]) + _dot(ys_ref[0], w_ref[D_POOL:, :])
    o_ref[0] = x_ref[0] + (1.0 + mod_ref[0][:, 2 * D_MODEL:]) * y


def _out0(yp, ys, x, mod, w_out):
    bsz, seq, d = x.shape
    tm = TM_OUT0
    half = pl.BlockSpec((1, tm, D_POOL), lambda b, s: (b, s, 0))
    row = pl.BlockSpec((1, tm, d), lambda b, s: (b, s, 0))
    return pl.pallas_call(
        _out0_kernel,
        grid=(bsz, seq // tm),
        in_specs=[
            half, half, row,
            pl.BlockSpec((1, 1, 3 * d), lambda b, s: (b, 0, 0)),
            pl.BlockSpec(w_out.shape, lambda b, s: (0, 0), pipeline_mode=pl.Buffered(1)),
        ],
        out_specs=row,
        out_shape=jax.ShapeDtypeStruct(x.shape, F32),
        compiler_params=pltpu.CompilerParams(
            dimension_semantics=("arbitrary", "arbitrary"),
            vmem_limit_bytes=VMEM_LIMIT_BYTES),
        name="out0",
    )(yp, ys, x, mod, w_out)


def _layer1_kernel(x_ref, mod_ref, g_ref, w_in_ref, cw_ref, cb_ref, w_out_ref, fg_ref,
                   o_ref, ext_ref, y_ref):
    s = pl.program_id(1)
    tm = x_ref.shape[1]
    pad = SUBLANES
    x = x_ref[0]
    mod = mod_ref[0]
    h = _modulated_norm(x, g_ref[...], mod).astype(BF16)

    @pl.when(s == 0)
    def _():
        ext_ref[0:pad, :] = jnp.zeros((pad, D_CONV), F32)

    @pl.when(s > 0)
    def _():
        ext_ref[0:pad, :] = ext_ref[tm:tm + pad, :]

    for c0 in range(0, D_CONV, CONV_CHUNK):
        cols = slice(c0, c0 + CONV_CHUNK)

        def proj(k):
            return _dot(h, w_in_ref[:, k * D_CONV + c0:k * D_CONV + c0 + CONV_CHUNK])

        u = proj(1) * proj(2)
        ext_ref[pad:, cols] = u
        conv = cb_ref[:, cols] + cw_ref[2:3, cols] * u
        for j in range(CONV_WIDTH - 1):
            shift = CONV_WIDTH - 1 - j
            conv = conv + cw_ref[j:j + 1, cols] * ext_ref[pad - shift:pad - shift + tm, cols]
        y_ref[:, cols] = (proj(0) * conv * _silu(proj(3))).astype(BF16)

    y = _dot(y_ref[...], w_out_ref[...])
    x2 = x + (1.0 + mod[:, 2 * D_MODEL:]) * y
    o_ref[0] = x2 * lax.rsqrt(jnp.mean(x2 * x2, axis=-1, keepdims=True) + EPS) * fg_ref[...]


def _layer1(x, mod, g, w_in, conv_w, conv_b, w_out, final_g):
    bsz, seq, d = x.shape
    tm = TM_L1
    row = pl.BlockSpec((1, tm, d), lambda b, s: (b, s, 0))
    const2 = lambda b, s: (0, 0)
    return pl.pallas_call(
        _layer1_kernel,
        grid=(bsz, seq // tm),
        in_specs=[
            row,
            pl.BlockSpec((1, 1, 3 * d), lambda b, s: (b, 0, 0)),
            pl.BlockSpec((1, d), const2),
            pl.BlockSpec(w_in.shape, const2, pipeline_mode=pl.Buffered(1)),
            pl.BlockSpec(conv_w.shape, const2),
            pl.BlockSpec((1, D_CONV), const2),
            pl.BlockSpec(w_out.shape, const2, pipeline_mode=pl.Buffered(1)),
            pl.BlockSpec((1, d), const2),
        ],
        out_specs=row,
        out_shape=jax.ShapeDtypeStruct(x.shape, F32),
        scratch_shapes=[pltpu.VMEM((tm + SUBLANES, D_CONV), F32),
                        pltpu.VMEM((tm, D_CONV), BF16)],
        compiler_params=pltpu.CompilerParams(
            dimension_semantics=("arbitrary", "arbitrary"),
            vmem_limit_bytes=VMEM_LIMIT_BYTES),
        name="layer1",
    )(x, mod, g, w_in, conv_w, conv_b, w_out, final_g)


def kernel(x, c, norm_g, ada_w, ada_b, even_w_in, pool_w, pool_scale, even_w_out,
           odd_w_in, conv_w, conv_b, odd_w_out, final_g):
    bsz = x.shape[0]
    mods = _adaln(c, ada_w, ada_b)
    mod0 = mods[0].reshape(bsz, 1, 3 * D_MODEL)
    mod1 = mods[1].reshape(bsz, 1, 3 * D_MODEL)

    yg_pool, q, k, v, sg = _in0(
        x, mod0, norm_g[0:1], even_w_in[0].astype(BF16), pool_w[0].astype(BF16), pool_scale[0:1])
    yg_sb = _attention(q, k, v, sg)
    x1 = _out0(yg_pool, yg_sb, x, mod0, even_w_out[0].astype(BF16))
    return _layer1(x1, mod1, norm_g[1:2], odd_w_in[0].astype(BF16), conv_w[0], conv_b[0:1],
                   odd_w_out[0].astype(BF16), final_g.reshape(1, D_MODEL))
```

```python
import functools

import jax
import jax.numpy as jnp
from jax import lax
from jax.experimental import pallas as pl
from jax.experimental.pallas import tpu as pltpu

D_MODEL = 1024
D_INNER = 2 * D_MODEL
D_POOL = D_INNER // 2
D_SB = D_INNER - D_POOL
POOL_WINDOWS = (2, 4, 8, 16)
POOL_GROUP = D_POOL // len(POOL_WINDOWS)
MAX_WINDOW = max(POOL_WINDOWS)
HEAD_DIM = 64
CONV_WIDTH = 3
D_CONV = D_INNER
EPS = 1e-6

LANES = 128
SUBLANES = 8
VMEM_LIMIT_BYTES = 56 * 1024 * 1024

TM_IN0 = 512
TM_OUT0 = 512
TM_L1 = 256
T_ATT = 256
ATT_HEADS = 2
CONV_CHUNK = 512

F32 = jnp.float32
BF16 = jnp.bfloat16

MASKED_LOG = -1e30
EXP_UNDERFLOW = -110.0


def _silu(x):
    return x * (1.0 / (1.0 + jnp.exp(-x)))


def _dot(a, b):
    return jnp.dot(a, b, preferred_element_type=F32)


def _modulated_norm(x, g, mod):
    y = x * lax.rsqrt(jnp.mean(x * x, axis=-1, keepdims=True) + EPS) * g
    return y * (1.0 + mod[:, D_MODEL:2 * D_MODEL]) + mod[:, :D_MODEL]


def _adaln_kernel(c_ref, w_ref, b_ref, o_ref):
    o_ref[0] = _dot(_silu(c_ref[...]), w_ref[0]) + b_ref[0]


def _adaln(c, ada_w, ada_b):
    depth, d, d3 = ada_w.shape
    bsz = c.shape[0]
    tn = D_MODEL
    return pl.pallas_call(
        _adaln_kernel,
        grid=(depth, d3 // tn),
        in_specs=[
            pl.BlockSpec((bsz, d), lambda i, j: (0, 0)),
            pl.BlockSpec((1, d, tn), lambda i, j: (i, 0, j)),
            pl.BlockSpec((1, 1, tn), lambda i, j: (i, 0, j)),
        ],
        out_specs=pl.BlockSpec((1, bsz, tn), lambda i, j: (i, 0, j)),
        out_shape=jax.ShapeDtypeStruct((depth, bsz, d3), F32),
        name="adaln",
    )(c, ada_w, ada_b.reshape(depth, 1, d3))


def _in0_kernel(x_ref, mod_ref, g_ref, w_ref, pw_ref, ps_ref,
                yg_ref, q_ref, k_ref, v_ref, sg_ref, ext_ref):
    s = pl.program_id(1)
    tm = x_ref.shape[1]
    h = _modulated_norm(x_ref[0], g_ref[...], mod_ref[0]).astype(BF16)

    @pl.when(s == 0)
    def _():
        ext_ref[0:MAX_WINDOW, :] = jnp.zeros((MAX_WINDOW, D_POOL), F32)

    @pl.when(s > 0)
    def _():
        ext_ref[0:MAX_WINDOW, :] = ext_ref[tm:tm + MAX_WINDOW, :]

    ext_ref[MAX_WINDOW:, :] = _dot(h, w_ref[:, 0:D_POOL])

    pos = s * tm + lax.broadcasted_iota(jnp.int32, (tm, 1), 0)
    gate0 = D_POOL + 3 * D_SB
    for gi, win in enumerate(POOL_WINDOWS):
        c0 = gi * POOL_GROUP
        cols = slice(c0, c0 + POOL_GROUP)
        u = ext_ref[MAX_WINDOW:, cols]
        acc = u
        for j in range(1, win):
            acc = acc + ext_ref[MAX_WINDOW - j:MAX_WINDOW - j + tm, cols]
        inv_cnt = 1.0 / jnp.minimum(pos + 1, win).astype(F32)
        p = acc * inv_cnt - u
        y = _dot(p.astype(BF16), pw_ref[gi]) * ps_ref[:, cols]
        gate = _dot(h, w_ref[:, gate0 + c0:gate0 + c0 + POOL_GROUP])
        yg_ref[0, :, cols] = (y * _silu(gate)).astype(BF16)

    q_ref[0] = (_dot(h, w_ref[:, D_POOL:D_POOL + D_SB]) * 0.125).astype(BF16)
    k_ref[0] = _dot(h, w_ref[:, D_POOL + D_SB:D_POOL + 2 * D_SB]).astype(BF16)
    v_ref[0] = _dot(h, w_ref[:, D_POOL + 2 * D_SB:D_POOL + 3 * D_SB]).astype(BF16)
    sg_ref[0] = _silu(_dot(h, w_ref[:, gate0 + D_POOL:])).astype(BF16)


def _in0(x, mod, g, w_in, pool_w, pool_scale):
    bsz, seq, d = x.shape
    tm = TM_IN0
    n_in = w_in.shape[1]
    row_spec = pl.BlockSpec((1, tm, D_SB), lambda b, s: (b, s, 0))
    out_sds = jax.ShapeDtypeStruct((bsz, seq, D_SB), BF16)
    return pl.pallas_call(
        _in0_kernel,
        grid=(bsz, seq // tm),
        in_specs=[
            pl.BlockSpec((1, tm, d), lambda b, s: (b, s, 0)),
            pl.BlockSpec((1, 1, 3 * d), lambda b, s: (b, 0, 0)),
            pl.BlockSpec((1, d), lambda b, s: (0, 0)),
            pl.BlockSpec((d, n_in), lambda b, s: (0, 0), pipeline_mode=pl.Buffered(1)),
            pl.BlockSpec(pool_w.shape, lambda b, s: (0, 0, 0), pipeline_mode=pl.Buffered(1)),
            pl.BlockSpec((1, D_POOL), lambda b, s: (0, 0)),
        ],
        out_specs=[row_spec] * 5,
        out_shape=[out_sds] * 5,
        scratch_shapes=[pltpu.VMEM((tm + MAX_WINDOW, D_POOL), F32)],
        compiler_params=pltpu.CompilerParams(
            dimension_semantics=("arbitrary", "arbitrary"),
            vmem_limit_bytes=VMEM_LIMIT_BYTES),
        name="in0",
    )(x, mod, g, w_in, pool_w, pool_scale)


ATT_DEPTH = 5
KIND_PLAIN, KIND_DIAGONAL, KIND_DEAD = 0, 1, 2
ST_NEXT_Q, ST_IDLE, ST_SKIP = 0, 1, 2


def _softplus(z):
    return jnp.maximum(z, 0.0) + jnp.log(1.0 + jnp.exp(-jnp.abs(z)))


def _attn_kernel(q_ref, k_ref, v_ref, sg_ref, o_ref,
                 acc_ref, qm_ref, z_ref, sp_ref, ls_ref, before_ref, after_ref, a_ref, run_ref,
                 bias_ref, tri_ref, pipe_ref, pos_ref, state_ref):
    t = T_ATT
    n_blocks = q_ref.shape[1] // t
    lane = lax.broadcasted_iota(jnp.int32, (1, LANES), 1)
    head_lanes = (lane < HEAD_DIM, lane >= HEAD_DIM)
    heads = range(ATT_HEADS)

    def slab(hd):
        return slice((hd // 2) * LANES, (hd // 2 + 1) * LANES)
    row = lax.broadcasted_iota(jnp.int32, (t, t), 0)
    col = lax.broadcasted_iota(jnp.int32, (t, t), 1)
    tri_ref[...] = jnp.where(row > col, -1.0, 0.0).astype(BF16)
    causal = col < row
    bias_ref[KIND_PLAIN] = jnp.zeros((t, t), F32)
    bias_ref[KIND_DIAGONAL] = jnp.where(causal, 0.0, MASKED_LOG)
    bias_ref[KIND_DEAD] = jnp.full((t, t), MASKED_LOG, F32)
    for ref in (acc_ref, z_ref, sp_ref, before_ref, after_ref, a_ref, run_ref):
        ref[...] = jnp.zeros_like(ref)
    ls_ref[...] = jnp.full(ls_ref.shape, MASKED_LOG, BF16)
    for hd in heads:
        q2 = q_ref[0, :, slab(hd)]
        qm_ref[hd] = jnp.where(head_lanes[hd % 2], q2, jnp.zeros_like(q2))

    def rows_of(blk):
        return pl.ds(pl.multiple_of(blk * t, t), t)

    for s in range(ATT_DEPTH):
        pipe_ref[s, 0] = 0
        pipe_ref[s, 1] = 0
        pipe_ref[s, 2] = KIND_DEAD
    for w in range(2):
        pos_ref[w, 0] = 0
        pos_ref[w, 1] = -1
        state_ref[ST_SKIP + w] = 0
    state_ref[ST_NEXT_Q] = 0
    state_ref[ST_IDLE] = 0

    def issue(walker):
        for s in range(ATT_DEPTH - 1, 0, -1):
            for c in range(3):
                pipe_ref[s, c] = pipe_ref[s - 1, c]
        qi, kj = pos_ref[walker, 0], pos_ref[walker, 1]
        next_q = state_ref[ST_NEXT_Q]
        fresh = jnp.logical_or(kj < 0, state_ref[ST_SKIP + walker] == 1)
        qi = jnp.where(fresh, next_q, qi)
        kj = jnp.where(fresh, next_q, kj)
        state_ref[ST_NEXT_Q] = jnp.where(fresh, jnp.minimum(next_q + 1, n_blocks), next_q)
        live = qi < n_blocks
        pipe_ref[0, 0] = jnp.where(live, qi, 0)
        pipe_ref[0, 1] = jnp.where(live, kj, 0)
        pipe_ref[0, 2] = jnp.where(live, jnp.where(kj == qi, KIND_DIAGONAL, KIND_PLAIN), KIND_DEAD)
        pos_ref[walker, 0] = qi
        pos_ref[walker, 1] = jnp.where(live, kj - 1, -1)
        state_ref[ST_IDLE] = jnp.where(live, 0, state_ref[ST_IDLE] + 1)

    def trip(par):
        prev = 1 - par
        issue(par)
        qi_0, kj_0 = pipe_ref[0, 0], pipe_ref[0, 1]
        kind_1 = pipe_ref[1, 2]
        qi_4, kj_4 = pipe_ref[4, 0], pipe_ref[4, 1]

        for hd in heads:
            acc_ref[qi_4, hd] += _dot(a_ref[prev, hd], v_ref[0, rows_of(kj_4), slab(hd)])
        for hd in heads:
            after = _dot(sp_ref[prev, hd], tri_ref[...]) + before_ref[prev, hd]
            after_ref[par, hd] = after.astype(BF16)
        for hd in heads:
            z_ref[par, hd] = lax.dot_general(
                qm_ref[hd, rows_of(qi_0), :], k_ref[0, rows_of(kj_0), slab(hd)],
                (((1,), (1,)), ((), ())), preferred_element_type=F32)

        for hd in heads:
            a_ref[par, hd] = jnp.exp(ls_ref[par, hd] + after_ref[prev, hd])
        start = jnp.where(kind_1 == KIND_PLAIN, 1 + prev, 0)
        highest = None
        for hd in heads:
            z = z_ref[prev, hd] + bias_ref[kind_1]
            sp = _softplus(z)
            ls_ref[par, hd] = (z - sp).astype(BF16)
            sp_ref[par, hd] = sp.astype(BF16)
            run = run_ref[start, hd]
            before_ref[par, hd] = run
            run = run - jnp.sum(sp, axis=1, keepdims=True)
            run_ref[1 + prev, hd] = run
            top = jnp.max(run)
            highest = top if highest is None else jnp.maximum(highest, top)
        state_ref[ST_SKIP + prev] = (highest < EXP_UNDERFLOW).astype(jnp.int32)

    def two_trips(_):
        trip(0)
        trip(1)
        return state_ref[ST_IDLE]

    lax.while_loop(lambda idle: idle < ATT_DEPTH - 1, two_trips, jnp.int32(0))

    for i in range(n_blocks):
        rows = slice(i * t, (i + 1) * t)
        for hd in range(0, ATT_HEADS, 2):
            o = jnp.where(head_lanes[0], acc_ref[i, hd], acc_ref[i, hd + 1])
            o_ref[0, rows, slab(hd)] = (o * sg_ref[0, rows, slab(hd)].astype(F32)).astype(BF16)


def _attention(q, k, v, sg):
    bsz, seq, d = q.shape
    t = T_ATT
    nh = ATT_HEADS
    width = nh * HEAD_DIM
    full = pl.BlockSpec((1, seq, width), lambda b, hg: (b, 0, hg))
    return pl.pallas_call(
        _attn_kernel,
        grid_spec=pltpu.PrefetchScalarGridSpec(
            num_scalar_prefetch=0,
            grid=(bsz, d // width),
            in_specs=[full, full, full, full],
            out_specs=full,
            scratch_shapes=[
                pltpu.VMEM((seq // t, nh, t, LANES), F32),
                pltpu.VMEM((nh, seq, LANES), BF16),
                pltpu.VMEM((2, nh, t, t), F32),
                pltpu.VMEM((2, nh, t, t), BF16),
                pltpu.VMEM((2, nh, t, t), BF16),
                pltpu.VMEM((2, nh, t, 1), F32),
                pltpu.VMEM((2, nh, t, t), BF16),
                pltpu.VMEM((2, nh, t, t), BF16),
                pltpu.VMEM((3, nh, t, 1), F32),
                pltpu.VMEM((3, t, t), F32),
                pltpu.VMEM((t, t), BF16),
                pltpu.SMEM((ATT_DEPTH, 3), jnp.int32),
                pltpu.SMEM((2, 2), jnp.int32),
                pltpu.SMEM((ST_SKIP + 2,), jnp.int32),
            ]),
        out_shape=jax.ShapeDtypeStruct((bsz, seq, d), BF16),
        compiler_params=pltpu.CompilerParams(
            dimension_semantics=("arbitrary", "arbitrary"),
            vmem_limit_bytes=VMEM_LIMIT_BYTES),
        name="attn",
    )(q, k, v, sg)


def _out0_kernel(yp_ref, ys_ref, x_ref, mod_ref, w_ref, o_ref):
    y = _dot(yp_ref[0], w_ref[0:D_POOL, :]) + _dot(ys_ref[0], w_ref[D_POOL:, :])
    o_ref[0] = x_ref[0] + (1.0 + mod_ref[0][:, 2 * D_MODEL:]) * y


def _out0(yp, ys, x, mod, w_out):
    bsz, seq, d = x.shape
    tm = TM_OUT0
    half = pl.BlockSpec((1, tm, D_POOL), lambda b, s: (b, s, 0))
    row = pl.BlockSpec((1, tm, d), lambda b, s: (b, s, 0))
    return pl.pallas_call(
        _out0_kernel,
        grid=(bsz, seq // tm),
        in_specs=[
            half, half, row,
            pl.BlockSpec((1, 1, 3 * d), lambda b, s: (b, 0, 0)),
            pl.BlockSpec(w_out.shape, lambda b, s: (0, 0), pipeline_mode=pl.Buffered(1)),
        ],
        out_specs=row,
        out_shape=jax.ShapeDtypeStruct(x.shape, F32),
        compiler_params=pltpu.CompilerParams(
            dimension_semantics=("arbitrary", "arbitrary"),
            vmem_limit_bytes=VMEM_LIMIT_BYTES),
        name="out0",
    )(yp, ys, x, mod, w_out)


def _layer1_kernel(x_ref, mod_ref, g_ref, w_in_ref, cw_ref, cb_ref, w_out_ref, fg_ref,
                   o_ref, ext_ref, y_ref):
    s = pl.program_id(1)
    tm = x_ref.shape[1]
    pad = SUBLANES
    x = x_ref[0]
    mod = mod_ref[0]
    h = _modulated_norm(x, g_ref[...], mod).astype(BF16)

    @pl.when(s == 0)
    def _():
        ext_ref[0:pad, :] = jnp.zeros((pad, D_CONV), F32)

    @pl.when(s > 0)
    def _():
        ext_ref[0:pad, :] = ext_ref[tm:tm + pad, :]

    for c0 in range(0, D_CONV, CONV_CHUNK):
        cols = slice(c0, c0 + CONV_CHUNK)

        def proj(k):
            return _dot(h, w_in_ref[:, k * D_CONV + c0:k * D_CONV + c0 + CONV_CHUNK])

        u = proj(1) * proj(2)
        ext_ref[pad:, cols] = u
        conv = cb_ref[:, cols] + cw_ref[2:3, cols] * u
        for j in range(CONV_WIDTH - 1):
            shift = CONV_WIDTH - 1 - j
            conv = conv + cw_ref[j:j + 1, cols] * ext_ref[pad - shift:pad - shift + tm, cols]
        y_ref[:, cols] = (proj(0) * conv * _silu(proj(3))).astype(BF16)

    y = _dot(y_ref[...], w_out_ref[...])
    x2 = x + (1.0 + mod[:, 2 * D_MODEL:]) * y
    o_ref[0] = x2 * lax.rsqrt(jnp.mean(x2 * x2, axis=-1, keepdims=True) + EPS) * fg_ref[...]


def _layer1(x, mod, g, w_in, conv_w, conv_b, w_out, final_g):
    bsz, seq, d = x.shape
    tm = TM_L1
    row = pl.BlockSpec((1, tm, d), lambda b, s: (b, s, 0))
    const2 = lambda b, s: (0, 0)
    return pl.pallas_call(
        _layer1_kernel,
        grid=(bsz, seq // tm),
        in_specs=[
            row,
            pl.BlockSpec((1, 1, 3 * d), lambda b, s: (b, 0, 0)),
            pl.BlockSpec((1, d), const2),
            pl.BlockSpec(w_in.shape, const2, pipeline_mode=pl.Buffered(1)),
            pl.BlockSpec(conv_w.shape, const2),
            pl.BlockSpec((1, D_CONV), const2),
            pl.BlockSpec(w_out.shape, const2, pipeline_mode=pl.Buffered(1)),
            pl.BlockSpec((1, d), const2),
        ],
        out_specs=row,
        out_shape=jax.ShapeDtypeStruct(x.shape, F32),
        scratch_shapes=[pltpu.VMEM((tm + SUBLANES, D_CONV), F32),
                        pltpu.VMEM((tm, D_CONV), BF16)],
        compiler_params=pltpu.CompilerParams(
            dimension_semantics=("arbitrary", "arbitrary"),
            vmem_limit_bytes=VMEM_LIMIT_BYTES),
        name="layer1",
    )(x, mod, g, w_in, conv_w, conv_b, w_out, final_g)


def kernel(x, c, norm_g, ada_w, ada_b, even_w_in, pool_w, pool_scale, even_w_out,
           odd_w_in, conv_w, conv_b, odd_w_out, final_g):
    bsz = x.shape[0]
    mods = _adaln(c, ada_w, ada_b)
    mod0 = mods[0].reshape(bsz, 1, 3 * D_MODEL)
    mod1 = mods[1].reshape(bsz, 1, 3 * D_MODEL)

    yg_pool, q, k, v, sg = _in0(
        x, mod0, norm_g[0:1], even_w_in[0].astype(BF16), pool_w[0].astype(BF16), pool_scale[0:1])
    yg_sb = _attention(q, k, v, sg)
    x1 = _out0(yg_pool, yg_sb, x, mod0, even_w_out[0].astype(BF16))
    return _layer1(x1, mod1, norm_g[1:2], odd_w_in[0].astype(BF16), conv_w[0], conv_b[0:1],
                   odd_w_out[0].astype(BF16), final_g.reshape(1, D_MODEL))
```

```python
import functools

import jax
import jax.numpy as jnp
from jax import lax
from jax.experimental import pallas as pl
from jax.experimental.pallas import tpu as pltpu

D_MODEL = 1024
D_INNER = 2 * D_MODEL
D_POOL = D_INNER // 2
D_SB = D_INNER - D_POOL
POOL_WINDOWS = (2, 4, 8, 16)
POOL_GROUP = D_POOL // len(POOL_WINDOWS)
MAX_WINDOW = max(POOL_WINDOWS)
HEAD_DIM = 64
CONV_WIDTH = 3
D_CONV = D_INNER
EPS = 1e-6

LANES = 128
SUBLANES = 8
VMEM_LIMIT_BYTES = 56 * 1024 * 1024

TM_IN0 = 512
TM_OUT0 = 512
TM_L1 = 512
T_ATT = 256
ATT_HEADS = 4
CONV_CHUNK = 512

F32 = jnp.float32
BF16 = jnp.bfloat16

MASKED_LOG = -1e30
EXP_UNDERFLOW = -110.0


def _silu(x):
    half = 0.5 * x
    return half + half * jnp.tanh(half)


def _dot(a, b):
    return jnp.dot(a, b, preferred_element_type=F32)


def _modulated_norm(x, g, mod):
    gain = g * (1.0 + mod[:, D_MODEL:2 * D_MODEL])
    return x * lax.rsqrt(jnp.mean(x * x, axis=-1, keepdims=True) + EPS) * gain + mod[:, :D_MODEL]


def _adaln_kernel(c_ref, w_ref, b_ref, o_ref):
    o_ref[0] = _dot(_silu(c_ref[...]), w_ref[0]) + b_ref[0]


def _adaln(c, ada_w, ada_b):
    depth, d, d3 = ada_w.shape
    bsz = c.shape[0]
    tn = D_MODEL
    return pl.pallas_call(
        _adaln_kernel,
        grid=(depth, d3 // tn),
        in_specs=[
            pl.BlockSpec((bsz, d), lambda i, j: (0, 0)),
            pl.BlockSpec((1, d, tn), lambda i, j: (i, 0, j)),
            pl.BlockSpec((1, 1, tn), lambda i, j: (i, 0, j)),
        ],
        out_specs=pl.BlockSpec((1, bsz, tn), lambda i, j: (i, 0, j)),
        out_shape=jax.ShapeDtypeStruct((depth, bsz, d3), F32),
        name="adaln",
    )(c, ada_w, ada_b.reshape(depth, 1, d3))


def _in0_kernel(x_ref, mod_ref, g_ref, w_ref, pw_ref, ps_ref,
                yg_ref, q_ref, k_ref, v_ref, sg_ref, ext_ref):
    s = pl.program_id(1)
    tm = x_ref.shape[1]
    h = _modulated_norm(x_ref[0], g_ref[...], mod_ref[0]).astype(BF16)

    @pl.when(s == 0)
    def _():
        ext_ref[0:MAX_WINDOW, :] = jnp.zeros((MAX_WINDOW, D_POOL), F32)

    @pl.when(s > 0)
    def _():
        ext_ref[0:MAX_WINDOW, :] = ext_ref[tm:tm + MAX_WINDOW, :]

    ext_ref[MAX_WINDOW:, :] = _dot(h, w_ref[:, 0:D_POOL])

    pos = s * tm + lax.broadcasted_iota(jnp.int32, (tm, 1), 0)
    gate0 = D_POOL + 3 * D_SB
    for gi, win in enumerate(POOL_WINDOWS):
        c0 = gi * POOL_GROUP
        cols = slice(c0, c0 + POOL_GROUP)
        acc = ext_ref[:, cols]
        k = 1
        while k < win:
            acc = acc + pltpu.roll(acc, k, axis=0)
            k *= 2
        u = ext_ref[MAX_WINDOW:, cols]
        inv_cnt = 1.0 / jnp.minimum(pos + 1, win).astype(F32)
        p = acc[MAX_WINDOW:] * inv_cnt - u
        y = _dot(p.astype(BF16), pw_ref[gi]) * ps_ref[:, cols]
        gate = _dot(h, w_ref[:, gate0 + c0:gate0 + c0 + POOL_GROUP])
        yg_ref[0, :, cols] = (y * _silu(gate)).astype(BF16)

    sg_ref[0] = _silu(_dot(h, w_ref[:, gate0 + D_POOL:])).astype(BF16)
    q_ref[0] = (_dot(h, w_ref[:, D_POOL:D_POOL + D_SB]) * 0.125).astype(BF16)
    k_ref[0] = _dot(h, w_ref[:, D_POOL + D_SB:D_POOL + 2 * D_SB]).astype(BF16)
    v_ref[0] = _dot(h, w_ref[:, D_POOL + 2 * D_SB:D_POOL + 3 * D_SB]).astype(BF16)


def _in0(x, mod, g, w_in, pool_w, pool_scale):
    bsz, seq, d = x.shape
    tm = TM_IN0
    n_in = w_in.shape[1]
    row_spec = pl.BlockSpec((1, tm, D_SB), lambda b, s: (b, s, 0))
    out_sds = jax.ShapeDtypeStruct((bsz, seq, D_SB), BF16)
    return pl.pallas_call(
        _in0_kernel,
        grid=(bsz, seq // tm),
        in_specs=[
            pl.BlockSpec((1, tm, d), lambda b, s: (b, s, 0)),
            pl.BlockSpec((1, 1, 3 * d), lambda b, s: (b, 0, 0)),
            pl.BlockSpec((1, d), lambda b, s: (0, 0)),
            pl.BlockSpec((d, n_in), lambda b, s: (0, 0), pipeline_mode=pl.Buffered(1)),
            pl.BlockSpec(pool_w.shape, lambda b, s: (0, 0, 0), pipeline_mode=pl.Buffered(1)),
            pl.BlockSpec((1, D_POOL), lambda b, s: (0, 0)),
        ],
        out_specs=[row_spec] * 5,
        out_shape=[out_sds] * 5,
        scratch_shapes=[pltpu.VMEM((tm + MAX_WINDOW, D_POOL), F32)],
        compiler_params=pltpu.CompilerParams(
            dimension_semantics=("arbitrary", "arbitrary"),
            vmem_limit_bytes=VMEM_LIMIT_BYTES),
        name="in0",
    )(x, mod, g, w_in, pool_w, pool_scale)


ATT_DEPTH = 5
KIND_PLAIN, KIND_DIAGONAL, KIND_DEAD = 0, 1, 2
ST_NEXT_Q, ST_IDLE, ST_SKIP = 0, 1, 2


def _softplus(z):
    return jnp.maximum(z, 0.0) + jnp.log(1.0 + jnp.exp(-jnp.abs(z)))


def _attn_kernel(q_ref, k_ref, v_ref, sg_ref, o_ref,
                 acc_ref, qm_ref, z_ref, sp_ref, ls_ref, before_ref, after_ref, a_ref, run_ref,
                 bias_ref, tri_ref, pipe_ref, pos_ref, state_ref):
    t = T_ATT
    n_blocks = q_ref.shape[1] // t
    lane = lax.broadcasted_iota(jnp.int32, (1, LANES), 1)
    head_lanes = (lane < HEAD_DIM, lane >= HEAD_DIM)
    heads = range(ATT_HEADS)

    def slab(hd):
        return slice((hd // 2) * LANES, (hd // 2 + 1) * LANES)
    row = lax.broadcasted_iota(jnp.int32, (t, t), 0)
    col = lax.broadcasted_iota(jnp.int32, (t, t), 1)
    tri_ref[...] = jnp.where(row > col, -1.0, 0.0).astype(BF16)
    causal = col < row
    bias_ref[KIND_PLAIN] = jnp.zeros((t, t), F32)
    bias_ref[KIND_DIAGONAL] = jnp.where(causal, 0.0, MASKED_LOG)
    bias_ref[KIND_DEAD] = jnp.full((t, t), MASKED_LOG, F32)
    for ref in (acc_ref, z_ref, sp_ref, before_ref, after_ref, a_ref, run_ref):
        ref[...] = jnp.zeros_like(ref)
    ls_ref[...] = jnp.full(ls_ref.shape, MASKED_LOG, BF16)
    for hd in heads:
        q2 = q_ref[0, :, slab(hd)]
        qm_ref[hd] = jnp.where(head_lanes[hd % 2], q2, jnp.zeros_like(q2))

    def rows_of(blk):
        return pl.ds(pl.multiple_of(blk * t, t), t)

    for s in range(ATT_DEPTH):
        pipe_ref[s, 0] = 0
        pipe_ref[s, 1] = 0
        pipe_ref[s, 2] = KIND_DEAD
    for w in range(2):
        pos_ref[w, 0] = 0
        pos_ref[w, 1] = -1
        state_ref[ST_SKIP + w] = 0
    state_ref[ST_NEXT_Q] = 0
    state_ref[ST_IDLE] = 0

    def issue(walker):
        for s in range(ATT_DEPTH - 1, 0, -1):
            for c in range(3):
                pipe_ref[s, c] = pipe_ref[s - 1, c]
        qi, kj = pos_ref[walker, 0], pos_ref[walker, 1]
        next_q = state_ref[ST_NEXT_Q]
        fresh = jnp.logical_or(kj < 0, state_ref[ST_SKIP + walker] == 1)
        qi = jnp.where(fresh, next_q, qi)
        kj = jnp.where(fresh, next_q, kj)
        state_ref[ST_NEXT_Q] = jnp.where(fresh, jnp.minimum(next_q + 1, n_blocks), next_q)
        live = qi < n_blocks
        pipe_ref[0, 0] = jnp.where(live, qi, 0)
        pipe_ref[0, 1] = jnp.where(live, kj, 0)
        pipe_ref[0, 2] = jnp.where(live, jnp.where(kj == qi, KIND_DIAGONAL, KIND_PLAIN), KIND_DEAD)
        pos_ref[walker, 0] = qi
        pos_ref[walker, 1] = jnp.where(live, kj - 1, -1)
        state_ref[ST_IDLE] = jnp.where(live, 0, state_ref[ST_IDLE] + 1)

    def trip(par):
        prev = 1 - par
        issue(par)
        qi_0, kj_0 = pipe_ref[0, 0], pipe_ref[0, 1]
        kind_1 = pipe_ref[1, 2]
        qi_4, kj_4 = pipe_ref[4, 0], pipe_ref[4, 1]

        for hd in heads:
            acc_ref[qi_4, hd] += _dot(a_ref[prev, hd], v_ref[0, rows_of(kj_4), slab(hd)])
        for hd in heads:
            after = _dot(sp_ref[prev, hd], tri_ref[...]) + before_ref[prev, hd]
            after_ref[par, hd] = after.astype(BF16)
        for hd in heads:
            z_ref[par, hd] = lax.dot_general(
                qm_ref[hd, rows_of(qi_0), :], k_ref[0, rows_of(kj_0), slab(hd)],
                (((1,), (1,)), ((), ())), preferred_element_type=F32)

        for hd in heads:
            a_ref[par, hd] = jnp.exp(ls_ref[par, hd] + after_ref[prev, hd])
        start = jnp.where(kind_1 == KIND_PLAIN, 1 + prev, 0)
        highest = None
        for hd in heads:
            z = z_ref[prev, hd] + bias_ref[kind_1]
            sp = _softplus(z)
            ls_ref[par, hd] = (z - sp).astype(BF16)
            sp_ref[par, hd] = sp.astype(BF16)
            run = run_ref[start, hd]
            before_ref[par, hd] = run
            run = run - jnp.sum(sp, axis=1, keepdims=True)
            run_ref[1 + prev, hd] = run
            top = jnp.max(run)
            highest = top if highest is None else jnp.maximum(highest, top)
        state_ref[ST_SKIP + prev] = (highest < EXP_UNDERFLOW).astype(jnp.int32)

    def two_trips(_):
        trip(0)
        trip(1)
        return state_ref[ST_IDLE]

    lax.while_loop(lambda idle: idle < ATT_DEPTH - 1, two_trips, jnp.int32(0))

    for i in range(n_blocks):
        rows = slice(i * t, (i + 1) * t)
        for hd in range(0, ATT_HEADS, 2):
            o = jnp.where(head_lanes[0], acc_ref[i, hd], acc_ref[i, hd + 1])
            o_ref[0, rows, slab(hd)] = (o * sg_ref[0, rows, slab(hd)].astype(F32)).astype(BF16)


def _attention(q, k, v, sg):
    bsz, seq, d = q.shape
    t = T_ATT
    nh = ATT_HEADS
    width = nh * HEAD_DIM
    full = pl.BlockSpec((1, seq, width), lambda b, hg: (b, 0, hg))
    return pl.pallas_call(
        _attn_kernel,
        grid_spec=pltpu.PrefetchScalarGridSpec(
            num_scalar_prefetch=0,
            grid=(bsz, d // width),
            in_specs=[full, full, full, full],
            out_specs=full,
            scratch_shapes=[
                pltpu.VMEM((seq // t, nh, t, LANES), F32),
                pltpu.VMEM((nh, seq, LANES), BF16),
                pltpu.VMEM((2, nh, t, t), F32),
                pltpu.VMEM((2, nh, t, t), BF16),
                pltpu.VMEM((2, nh, t, t), BF16),
                pltpu.VMEM((2, nh, t, 1), F32),
                pltpu.VMEM((2, nh, t, t), BF16),
                pltpu.VMEM((2, nh, t, t), BF16),
                pltpu.VMEM((3, nh, t, 1), F32),
                pltpu.VMEM((3, t, t), F32),
                pltpu.VMEM((t, t), BF16),
                pltpu.SMEM((ATT_DEPTH, 3), jnp.int32),
                pltpu.SMEM((2, 2), jnp.int32),
                pltpu.SMEM((ST_SKIP + 2,), jnp.int32),
            ]),
        out_shape=jax.ShapeDtypeStruct((bsz, seq, d), BF16),
        compiler_params=pltpu.CompilerParams(
            dimension_semantics=("arbitrary", "arbitrary"),
            vmem_limit_bytes=VMEM_LIMIT_BYTES),
        name="attn",
    )(q, k, v, sg)


def _out0_kernel(yp_ref, ys_ref, x_ref, mod_ref, w_ref, o_ref):
    y = _dot(yp_ref[0], w_ref[0:D_POOL, :]) + _dot(ys_ref[0], w_ref[D_POOL:, :])
    o_ref[0] = x_ref[0] + (1.0 + mod_ref[0][:, 2 * D_MODEL:]) * y


def _out0(yp, ys, x, mod, w_out):
    bsz, seq, d = x.shape
    tm = TM_OUT0
    half = pl.BlockSpec((1, tm, D_POOL), lambda b, s: (b, s, 0))
    row = pl.BlockSpec((1, tm, d), lambda b, s: (b, s, 0))
    return pl.pallas_call(
        _out0_kernel,
        grid=(bsz, seq // tm),
        in_specs=[
            half, half, row,
            pl.BlockSpec((1, 1, 3 * d), lambda b, s: (b, 0, 0)),
            pl.BlockSpec(w_out.shape, lambda b, s: (0, 0), pipeline_mode=pl.Buffered(1)),
        ],
        out_specs=row,
        out_shape=jax.ShapeDtypeStruct(x.shape, F32),
        compiler_params=pltpu.CompilerParams(
            dimension_semantics=("arbitrary", "arbitrary"),
            vmem_limit_bytes=VMEM_LIMIT_BYTES),
        name="out0",
    )(yp, ys, x, mod, w_out)


def _layer1_kernel(x_ref, mod_ref, g_ref, w_in_ref, cw_ref, cb_ref, w_out_ref, fg_ref,
                   o_ref, ext_ref, y_ref):
    s = pl.program_id(1)
    tm = x_ref.shape[1]
    pad = SUBLANES
    x = x_ref[0]
    mod = mod_ref[0]
    h = _modulated_norm(x, g_ref[...], mod).astype(BF16)

    @pl.when(s == 0)
    def _():
        ext_ref[0:pad, :] = jnp.zeros((pad, D_CONV), F32)

    @pl.when(s > 0)
    def _():
        ext_ref[0:pad, :] = ext_ref[tm:tm + pad, :]

    for c0 in range(0, D_CONV, CONV_CHUNK):
        cols = slice(c0, c0 + CONV_CHUNK)

        def proj(k):
            return _dot(h, w_in_ref[:, k * D_CONV + c0:k * D_CONV + c0 + CONV_CHUNK])

        u = proj(1) * proj(2)
        ext_ref[pad:, cols] = u
        conv = cb_ref[:, cols] + cw_ref[2:3, cols] * u
        for j in range(CONV_WIDTH - 1):
            shift = CONV_WIDTH - 1 - j
            conv = conv + cw_ref[j:j + 1, cols] * ext_ref[pad - shift:pad - shift + tm, cols]
        y_ref[:, cols] = (proj(0) * conv * _silu(proj(3))).astype(BF16)

    y = _dot(y_ref[...], w_out_ref[...])
    x2 = x + (1.0 + mod[:, 2 * D_MODEL:]) * y
    o_ref[0] = x2 * lax.rsqrt(jnp.mean(x2 * x2, axis=-1, keepdims=True) + EPS) * fg_ref[...]


def _layer1(x, mod, g, w_in, conv_w, conv_b, w_out, final_g):
    bsz, seq, d = x.shape
    tm = TM_L1
    row = pl.BlockSpec((1, tm, d), lambda b, s: (b, s, 0))
    const2 = lambda b, s: (0, 0)
    return pl.pallas_call(
        _layer1_kernel,
        grid=(bsz, seq // tm),
        in_specs=[
            row,
            pl.BlockSpec((1, 1, 3 * d), lambda b, s: (b, 0, 0)),
            pl.BlockSpec((1, d), const2),
            pl.BlockSpec(w_in.shape, const2, pipeline_mode=pl.Buffered(1)),
            pl.BlockSpec(conv_w.shape, const2),
            pl.BlockSpec((1, D_CONV), const2),
            pl.BlockSpec(w_out.shape, const2, pipeline_mode=pl.Buffered(1)),
            pl.BlockSpec((1, d), const2),
        ],
        out_specs=row,
        out_shape=jax.ShapeDtypeStruct(x.shape, F32),
        scratch_shapes=[pltpu.VMEM((tm + SUBLANES, D_CONV), F32),
                        pltpu.VMEM((tm, D_CONV), BF16)],
        compiler_params=pltpu.CompilerParams(
            dimension_semantics=("arbitrary", "arbitrary"),
            vmem_limit_bytes=VMEM_LIMIT_BYTES),
        name="layer1",
    )(x, mod, g, w_in, conv_w, conv_b, w_out, final_g)


def kernel(x, c, norm_g, ada_w, ada_b, even_w_in, pool_w, pool_scale, even_w_out,
           odd_w_in, conv_w, conv_b, odd_w_out, final_g):
    bsz = x.shape[0]
    mods = _adaln(c, ada_w, ada_b)
    mod0 = mods[0].reshape(bsz, 1, 3 * D_MODEL)
    mod1 = mods[1].reshape(bsz, 1, 3 * D_MODEL)

    yg_pool, q, k, v, sg = _in0(
        x, mod0, norm_g[0:1], even_w_in[0].astype(BF16), pool_w[0].astype(BF16), pool_scale[0:1])
    yg_sb = _attention(q, k, v, sg)
    x1 = _out0(yg_pool, yg_sb, x, mod0, even_w_out[0].astype(BF16))
    return _layer1(x1, mod1, norm_g[1:2], odd_w_in[0].astype(BF16), conv_w[0], conv_b[0:1],
                   odd_w_out[0].astype(BF16), final_g.reshape(1, D_MODEL))
```

```python
import functools

import jax
import jax.numpy as jnp
from jax import lax
from jax.experimental import pallas as pl
from jax.experimental.pallas import tpu as pltpu

D_MODEL = 1024
D_INNER = 2 * D_MODEL
D_POOL = D_INNER // 2
D_SB = D_INNER - D_POOL
POOL_WINDOWS = (2, 4, 8, 16)
POOL_GROUP = D_POOL // len(POOL_WINDOWS)
MAX_WINDOW = max(POOL_WINDOWS)
HEAD_DIM = 64
CONV_WIDTH = 3
D_CONV = D_INNER
EPS = 1e-6

LANES = 128
SUBLANES = 8
VMEM_LIMIT_BYTES = 56 * 1024 * 1024

TM_IN0 = 512
TM_OUT0 = 512
TM_L1 = 512
T_ATT = 256
ATT_HEADS = 4
CONV_CHUNK = 512
L1_ROW_GROUPS = 2
IN0_ROW_GROUPS = 2

F32 = jnp.float32
BF16 = jnp.bfloat16

MASKED_LOG = -1e30
EXP_UNDERFLOW = -110.0


def _silu(x):
    half = 0.5 * x
    return half + half * jnp.tanh(half)


def _dot(a, b):
    return jnp.dot(a, b, preferred_element_type=F32)


def _modulated_norm(x, g, mod):
    gain = g * (1.0 + mod[:, D_MODEL:2 * D_MODEL])
    return x * lax.rsqrt(jnp.mean(x * x, axis=-1, keepdims=True) + EPS) * gain + mod[:, :D_MODEL]


def _adaln_kernel(c_ref, w_ref, b_ref, o_ref):
    o_ref[0] = _dot(_silu(c_ref[...]), w_ref[0]) + b_ref[0]


def _adaln(c, ada_w, ada_b):
    depth, d, d3 = ada_w.shape
    bsz = c.shape[0]
    tn = D_MODEL
    return pl.pallas_call(
        _adaln_kernel,
        grid=(depth, d3 // tn),
        in_specs=[
            pl.BlockSpec((bsz, d), lambda i, j: (0, 0)),
            pl.BlockSpec((1, d, tn), lambda i, j: (i, 0, j)),
            pl.BlockSpec((1, 1, tn), lambda i, j: (i, 0, j)),
        ],
        out_specs=pl.BlockSpec((1, bsz, tn), lambda i, j: (i, 0, j)),
        out_shape=jax.ShapeDtypeStruct((depth, bsz, d3), F32),
        name="adaln",
    )(c, ada_w, ada_b.reshape(depth, 1, d3))


def _in0_kernel(x_ref, mod_ref, g_ref, w_ref, pw_ref, ps_ref,
                yg_ref, q_ref, k_ref, v_ref, sg_ref, ext_ref):
    s = pl.program_id(1)
    tm = x_ref.shape[1]

    @pl.when(s == 0)
    def _():
        ext_ref[0:MAX_WINDOW, :] = jnp.zeros((MAX_WINDOW, D_POOL), F32)

    @pl.when(s > 0)
    def _():
        ext_ref[0:MAX_WINDOW, :] = ext_ref[tm:tm + MAX_WINDOW, :]

    tg = tm // IN0_ROW_GROUPS
    gate0 = D_POOL + 3 * D_SB
    for part in range(IN0_ROW_GROUPS):
        r0 = part * tg
        rows = slice(r0, r0 + tg)
        h = _modulated_norm(x_ref[0, rows, :], g_ref[...], mod_ref[0]).astype(BF16)
        ext_ref[MAX_WINDOW + r0:MAX_WINDOW + r0 + tg, :] = _dot(h, w_ref[:, 0:D_POOL])

        pos = s * tm + r0 + lax.broadcasted_iota(jnp.int32, (tg, 1), 0)
        for gi, win in enumerate(POOL_WINDOWS):
            c0 = gi * POOL_GROUP
            cols = slice(c0, c0 + POOL_GROUP)
            acc = ext_ref[r0:r0 + tg + MAX_WINDOW, cols]
            k = 1
            while k < win:
                acc = acc + pltpu.roll(acc, k, axis=0)
                k *= 2
            u = ext_ref[MAX_WINDOW + r0:MAX_WINDOW + r0 + tg, cols]
            inv_cnt = 1.0 / jnp.minimum(pos + 1, win).astype(F32)
            p = acc[MAX_WINDOW:] * inv_cnt - u
            y = _dot(p.astype(BF16), pw_ref[gi]) * ps_ref[:, cols]
            gate = _dot(h, w_ref[:, gate0 + c0:gate0 + c0 + POOL_GROUP])
            yg_ref[0, rows, cols] = (y * _silu(gate)).astype(BF16)

        sg_ref[0, rows, :] = _silu(_dot(h, w_ref[:, gate0 + D_POOL:])).astype(BF16)
        q_ref[0, rows, :] = (_dot(h, w_ref[:, D_POOL:D_POOL + D_SB]) * 0.125).astype(BF16)
        k_ref[0, rows, :] = _dot(h, w_ref[:, D_POOL + D_SB:D_POOL + 2 * D_SB]).astype(BF16)
        v_ref[0, rows, :] = _dot(h, w_ref[:, D_POOL + 2 * D_SB:D_POOL + 3 * D_SB]).astype(BF16)


def _in0(x, mod, g, w_in, pool_w, pool_scale):
    bsz, seq, d = x.shape
    tm = TM_IN0
    n_in = w_in.shape[1]
    row_spec = pl.BlockSpec((1, tm, D_SB), lambda b, s: (b, s, 0))
    out_sds = jax.ShapeDtypeStruct((bsz, seq, D_SB), BF16)
    return pl.pallas_call(
        _in0_kernel,
        grid=(bsz, seq // tm),
        in_specs=[
            pl.BlockSpec((1, tm, d), lambda b, s: (b, s, 0)),
            pl.BlockSpec((1, 1, 3 * d), lambda b, s: (b, 0, 0)),
            pl.BlockSpec((1, d), lambda b, s: (0, 0)),
            pl.BlockSpec((d, n_in), lambda b, s: (0, 0), pipeline_mode=pl.Buffered(1)),
            pl.BlockSpec(pool_w.shape, lambda b, s: (0, 0, 0), pipeline_mode=pl.Buffered(1)),
            pl.BlockSpec((1, D_POOL), lambda b, s: (0, 0)),
        ],
        out_specs=[row_spec] * 5,
        out_shape=[out_sds] * 5,
        scratch_shapes=[pltpu.VMEM((tm + MAX_WINDOW, D_POOL), F32)],
        compiler_params=pltpu.CompilerParams(
            dimension_semantics=("arbitrary", "arbitrary"),
            vmem_limit_bytes=VMEM_LIMIT_BYTES),
        name="in0",
    )(x, mod, g, w_in, pool_w, pool_scale)


ATT_DEPTH = 5
KIND_PLAIN, KIND_DIAGONAL, KIND_DEAD = 0, 1, 2
ST_NEXT_Q, ST_IDLE, ST_SKIP = 0, 1, 2


def _softplus(z):
    return jnp.maximum(z, 0.0) + jnp.log(1.0 + jnp.exp(-jnp.abs(z)))


def _attn_kernel(q_ref, k_ref, v_ref, sg_ref, o_ref,
                 acc_ref, qm_ref, z_ref, sp_ref, ls_ref, before_ref, after_ref, a_ref, run_ref,
                 bias_ref, tri_ref, pipe_ref, pos_ref, state_ref):
    t = T_ATT
    n_blocks = q_ref.shape[1] // t
    lane = lax.broadcasted_iota(jnp.int32, (1, LANES), 1)
    head_lanes = (lane < HEAD_DIM, lane >= HEAD_DIM)
    heads = range(ATT_HEADS)

    def slab(hd):
        return slice((hd // 2) * LANES, (hd // 2 + 1) * LANES)
    row = lax.broadcasted_iota(jnp.int32, (t, t), 0)
    col = lax.broadcasted_iota(jnp.int32, (t, t), 1)
    tri_ref[...] = jnp.where(row > col, -1.0, 0.0).astype(BF16)
    causal = col < row
    bias_ref[KIND_PLAIN] = jnp.zeros((t, t), F32)
    bias_ref[KIND_DIAGONAL] = jnp.where(causal, 0.0, MASKED_LOG)
    bias_ref[KIND_DEAD] = jnp.full((t, t), MASKED_LOG, F32)
    for ref in (acc_ref, z_ref, sp_ref, before_ref, after_ref, a_ref, run_ref):
        ref[...] = jnp.zeros_like(ref)
    ls_ref[...] = jnp.full(ls_ref.shape, MASKED_LOG, BF16)
    for hd in heads:
        q2 = q_ref[0, :, slab(hd)]
        qm_ref[hd] = jnp.where(head_lanes[hd % 2], q2, jnp.zeros_like(q2))

    def rows_of(blk):
        return pl.ds(pl.multiple_of(blk * t, t), t)

    for s in range(ATT_DEPTH):
        pipe_ref[s, 0] = 0
        pipe_ref[s, 1] = 0
        pipe_ref[s, 2] = KIND_DEAD
    for w in range(2):
        pos_ref[w, 0] = 0
        pos_ref[w, 1] = -1
        state_ref[ST_SKIP + w] = 0
    state_ref[ST_NEXT_Q] = 0
    state_ref[ST_IDLE] = 0

    def advance():
        for s in range(ATT_DEPTH - 1, 0, -1):
            for c in range(3):
                pipe_ref[s, c] = pipe_ref[s - 1, c]

    def issue(walker):
        advance()
        qi, kj = pos_ref[walker, 0], pos_ref[walker, 1]
        next_q = state_ref[ST_NEXT_Q]
        fresh = jnp.logical_or(kj < 0, state_ref[ST_SKIP + walker] == 1)
        qi = jnp.where(fresh, next_q, qi)
        kj = jnp.where(fresh, next_q, kj)
        state_ref[ST_NEXT_Q] = jnp.where(fresh, jnp.minimum(next_q + 1, n_blocks), next_q)
        live = qi < n_blocks
        pipe_ref[0, 0] = jnp.where(live, qi, 0)
        pipe_ref[0, 1] = jnp.where(live, kj, 0)
        pipe_ref[0, 2] = jnp.where(live, jnp.where(kj == qi, KIND_DIAGONAL, KIND_PLAIN), KIND_DEAD)
        pos_ref[walker, 0] = qi
        pos_ref[walker, 1] = jnp.where(live, kj - 1, -1)
        state_ref[ST_IDLE] = jnp.where(live, 0, state_ref[ST_IDLE] + 1)

    def trip(par, first_stage=0):
        prev = 1 - par
        if first_stage == 0:
            issue(par)
        else:
            advance()
        qi_0, kj_0 = pipe_ref[0, 0], pipe_ref[0, 1]
        kind_1 = pipe_ref[1, 2]
        qi_4, kj_4 = pipe_ref[4, 0], pipe_ref[4, 1]

        for hd in heads:
            acc_ref[qi_4, hd] += _dot(a_ref[prev, hd], v_ref[0, rows_of(kj_4), slab(hd)])
        if first_stage <= 2:
            for hd in heads:
                after = _dot(sp_ref[prev, hd], tri_ref[...]) + before_ref[prev, hd]
                after_ref[par, hd] = after.astype(BF16)
        if first_stage <= 0:
            for hd in heads:
                z_ref[par, hd] = lax.dot_general(
                    qm_ref[hd, rows_of(qi_0), :], k_ref[0, rows_of(kj_0), slab(hd)],
                    (((1,), (1,)), ((), ())), preferred_element_type=F32)

        if first_stage <= 3:
            for hd in heads:
                a_ref[par, hd] = jnp.exp(ls_ref[par, hd] + after_ref[prev, hd])
        if first_stage > 1:
            return
        start = jnp.where(kind_1 == KIND_PLAIN, 1 + prev, 0)
        highest = None
        for hd in heads:
            z = z_ref[prev, hd] + bias_ref[kind_1]
            sp = _softplus(z)
            ls_ref[par, hd] = (z - sp).astype(BF16)
            sp_ref[par, hd] = sp.astype(BF16)
            run = run_ref[start, hd]
            before_ref[par, hd] = run
            run = run - jnp.sum(sp, axis=1, keepdims=True)
            run_ref[1 + prev, hd] = run
            top = jnp.max(run)
            highest = top if highest is None else jnp.maximum(highest, top)
        state_ref[ST_SKIP + prev] = (highest < EXP_UNDERFLOW).astype(jnp.int32)

    def two_trips(_):
        trip(0)
        trip(1)
        return state_ref[ST_IDLE]

    lax.while_loop(lambda idle: idle < 2, two_trips, jnp.int32(0))
    trip(0, first_stage=3)
    trip(1, first_stage=4)

    for i in range(n_blocks):
        rows = slice(i * t, (i + 1) * t)
        for hd in range(0, ATT_HEADS, 2):
            o = jnp.where(head_lanes[0], acc_ref[i, hd], acc_ref[i, hd + 1])
            o_ref[0, rows, slab(hd)] = (o * sg_ref[0, rows, slab(hd)].astype(F32)).astype(BF16)


def _attention(q, k, v, sg):
    bsz, seq, d = q.shape
    t = T_ATT
    nh = ATT_HEADS
    width = nh * HEAD_DIM
    full = pl.BlockSpec((1, seq, width), lambda b, hg: (b, 0, hg))
    return pl.pallas_call(
        _attn_kernel,
        grid_spec=pltpu.PrefetchScalarGridSpec(
            num_scalar_prefetch=0,
            grid=(bsz, d // width),
            in_specs=[full, full, full, full],
            out_specs=full,
            scratch_shapes=[
                pltpu.VMEM((seq // t, nh, t, LANES), F32),
                pltpu.VMEM((nh, seq, LANES), BF16),
                pltpu.VMEM((2, nh, t, t), F32),
                pltpu.VMEM((2, nh, t, t), BF16),
                pltpu.VMEM((2, nh, t, t), BF16),
                pltpu.VMEM((2, nh, t, 1), F32),
                pltpu.VMEM((2, nh, t, t), BF16),
                pltpu.VMEM((2, nh, t, t), BF16),
                pltpu.VMEM((3, nh, t, 1), F32),
                pltpu.VMEM((3, t, t), F32),
                pltpu.VMEM((t, t), BF16),
                pltpu.SMEM((ATT_DEPTH, 3), jnp.int32),
                pltpu.SMEM((2, 2), jnp.int32),
                pltpu.SMEM((ST_SKIP + 2,), jnp.int32),
            ]),
        out_shape=jax.ShapeDtypeStruct((bsz, seq, d), BF16),
        compiler_params=pltpu.CompilerParams(
            dimension_semantics=("arbitrary", "arbitrary"),
            vmem_limit_bytes=VMEM_LIMIT_BYTES),
        name="attn",
    )(q, k, v, sg)


def _out0_kernel(yp_ref, ys_ref, x_ref, mod_ref, w_ref, o_ref):
    y = _dot(yp_ref[0], w_ref[0:D_POOL, :]) + _dot(ys_ref[0], w_ref[D_POOL:, :])
    o_ref[0] = x_ref[0] + (1.0 + mod_ref[0][:, 2 * D_MODEL:]) * y


def _out0(yp, ys, x, mod, w_out):
    bsz, seq, d = x.shape
    tm = TM_OUT0
    half = pl.BlockSpec((1, tm, D_POOL), lambda b, s: (b, s, 0))
    row = pl.BlockSpec((1, tm, d), lambda b, s: (b, s, 0))
    return pl.pallas_call(
        _out0_kernel,
        grid=(bsz, seq // tm),
        in_specs=[
            half, half, row,
            pl.BlockSpec((1, 1, 3 * d), lambda b, s: (b, 0, 0)),
            pl.BlockSpec(w_out.shape, lambda b, s: (0, 0), pipeline_mode=pl.Buffered(1)),
        ],
        out_specs=row,
        out_shape=jax.ShapeDtypeStruct(x.shape, F32),
        compiler_params=pltpu.CompilerParams(
            dimension_semantics=("arbitrary", "arbitrary"),
            vmem_limit_bytes=VMEM_LIMIT_BYTES),
        name="out0",
    )(yp, ys, x, mod, w_out)


def _layer1_kernel(x_ref, mod_ref, g_ref, w_in_ref, cw_ref, cb_ref, w_out_ref, fg_ref,
                   o_ref, ext_ref, y_ref):
    s = pl.program_id(1)
    tm = x_ref.shape[1]
    pad = SUBLANES
    mod = mod_ref[0]

    @pl.when(s == 0)
    def _():
        ext_ref[0:pad, :] = jnp.zeros((pad, D_CONV), F32)

    @pl.when(s > 0)
    def _():
        ext_ref[0:pad, :] = ext_ref[tm:tm + pad, :]

    tg = tm // L1_ROW_GROUPS
    for part in range(L1_ROW_GROUPS):
        r0 = part * tg
        x = x_ref[0, r0:r0 + tg, :]
        h = _modulated_norm(x, g_ref[...], mod).astype(BF16)
        for c0 in range(0, D_CONV, CONV_CHUNK):
            cols = slice(c0, c0 + CONV_CHUNK)

            def proj(k):
                return _dot(h, w_in_ref[:, k * D_CONV + c0:k * D_CONV + c0 + CONV_CHUNK])

            u = proj(1) * proj(2)
            ext_ref[pad + r0:pad + r0 + tg, cols] = u
            conv = cb_ref[:, cols] + cw_ref[2:3, cols] * u
            for j in range(CONV_WIDTH - 1):
                shift = CONV_WIDTH - 1 - j
                conv = conv + cw_ref[j:j + 1, cols] * ext_ref[pad + r0 - shift:pad + r0 - shift + tg, cols]
            y_ref[r0:r0 + tg, cols] = (proj(0) * conv * _silu(proj(3))).astype(BF16)

        y = _dot(y_ref[r0:r0 + tg, :], w_out_ref[...])
        x2 = x + (1.0 + mod[:, 2 * D_MODEL:]) * y
        o_ref[0, r0:r0 + tg, :] = (
            x2 * lax.rsqrt(jnp.mean(x2 * x2, axis=-1, keepdims=True) + EPS) * fg_ref[...])


def _layer1(x, mod, g, w_in, conv_w, conv_b, w_out, final_g):
    bsz, seq, d = x.shape
    tm = TM_L1
    row = pl.BlockSpec((1, tm, d), lambda b, s: (b, s, 0))
    const2 = lambda b, s: (0, 0)
    return pl.pallas_call(
        _layer1_kernel,
        grid=(bsz, seq // tm),
        in_specs=[
            row,
            pl.BlockSpec((1, 1, 3 * d), lambda b, s: (b, 0, 0)),
            pl.BlockSpec((1, d), const2),
            pl.BlockSpec(w_in.shape, const2, pipeline_mode=pl.Buffered(1)),
            pl.BlockSpec(conv_w.shape, const2),
            pl.BlockSpec((1, D_CONV), const2),
            pl.BlockSpec(w_out.shape, const2, pipeline_mode=pl.Buffered(1)),
            pl.BlockSpec((1, d), const2),
        ],
        out_specs=row,
        out_shape=jax.ShapeDtypeStruct(x.shape, F32),
        scratch_shapes=[pltpu.VMEM((tm + SUBLANES, D_CONV), F32),
                        pltpu.VMEM((tm, D_CONV), BF16)],
        compiler_params=pltpu.CompilerParams(
            dimension_semantics=("arbitrary", "arbitrary"),
            vmem_limit_bytes=VMEM_LIMIT_BYTES),
        name="layer1",
    )(x, mod, g, w_in, conv_w, conv_b, w_out, final_g)


def kernel(x, c, norm_g, ada_w, ada_b, even_w_in, pool_w, pool_scale, even_w_out,
           odd_w_in, conv_w, conv_b, odd_w_out, final_g):
    bsz = x.shape[0]
    mods = _adaln(c, ada_w, ada_b)
    mod0 = mods[0].reshape(bsz, 1, 3 * D_MODEL)
    mod1 = mods[1].reshape(bsz, 1, 3 * D_MODEL)

    yg_pool, q, k, v, sg = _in0(
        x, mod0, norm_g[0:1], even_w_in[0].astype(BF16), pool_w[0].astype(BF16), pool_scale[0:1])
    yg_sb = _attention(q, k, v, sg)
    x1 = _out0(yg_pool, yg_sb, x, mod0, even_w_out[0].astype(BF16))
    return _layer1(x1, mod1, norm_g[1:2], odd_w_in[0].astype(BF16), conv_w[0], conv_b[0:1],
                   odd_w_out[0].astype(BF16), final_g.reshape(1, D_MODEL))
```

```python
import functools

import jax
import jax.numpy as jnp
from jax import lax
from jax.experimental import pallas as pl
from jax.experimental.pallas import tpu as pltpu

D_MODEL = 1024
D_INNER = 2 * D_MODEL
D_POOL = D_INNER // 2
D_SB = D_INNER - D_POOL
POOL_WINDOWS = (2, 4, 8, 16)
POOL_GROUP = D_POOL // len(POOL_WINDOWS)
MAX_WINDOW = max(POOL_WINDOWS)
HEAD_DIM = 64
CONV_WIDTH = 3
D_CONV = D_INNER
EPS = 1e-6

LANES = 128
SUBLANES = 8
VMEM_LIMIT_BYTES = 56 * 1024 * 1024

TM_IN0 = 512
TM_OUT0 = 1024
TM_L1 = 512
T_ATT = 256
ATT_HEADS = 4
CONV_CHUNK = 512
L1_ROW_GROUPS = 2
IN0_ROW_GROUPS = 2

F32 = jnp.float32
BF16 = jnp.bfloat16

MASKED_LOG = -1e30
EXP_UNDERFLOW = -110.0


def _silu(x):
    half = 0.5 * x
    return half + half * jnp.tanh(half)


def _dot(a, b):
    return jnp.dot(a, b, preferred_element_type=F32)


def _modulated_norm(x, g, mod):
    gain = g * (1.0 + mod[:, D_MODEL:2 * D_MODEL])
    return x * lax.rsqrt(jnp.mean(x * x, axis=-1, keepdims=True) + EPS) * gain + mod[:, :D_MODEL]


def _adaln_kernel(c_ref, w_ref, b_ref, o_ref):
    o_ref[0] = _dot(_silu(c_ref[...]), w_ref[0]) + b_ref[0]


def _adaln(c, ada_w, ada_b):
    depth, d, d3 = ada_w.shape
    bsz = c.shape[0]
    tn = D_MODEL
    return pl.pallas_call(
        _adaln_kernel,
        grid=(depth, d3 // tn),
        in_specs=[
            pl.BlockSpec((bsz, d), lambda i, j: (0, 0)),
            pl.BlockSpec((1, d, tn), lambda i, j: (i, 0, j)),
            pl.BlockSpec((1, 1, tn), lambda i, j: (i, 0, j)),
        ],
        out_specs=pl.BlockSpec((1, bsz, tn), lambda i, j: (i, 0, j)),
        out_shape=jax.ShapeDtypeStruct((depth, bsz, d3), F32),
        name="adaln",
    )(c, ada_w, ada_b.reshape(depth, 1, d3))


def _in0_kernel(x_ref, mod_ref, g_ref, w_ref, pw_ref, ps_ref,
                yg_ref, q_ref, k_ref, v_ref, sg_ref, ext_ref):
    s = pl.program_id(1)
    tm = x_ref.shape[1]

    @pl.when(s == 0)
    def _():
        ext_ref[0:MAX_WINDOW, :] = jnp.zeros((MAX_WINDOW, D_POOL), F32)

    @pl.when(s > 0)
    def _():
        ext_ref[0:MAX_WINDOW, :] = ext_ref[tm:tm + MAX_WINDOW, :]

    tg = tm // IN0_ROW_GROUPS
    gate0 = D_POOL + 3 * D_SB
    for part in range(IN0_ROW_GROUPS):
        r0 = part * tg
        rows = slice(r0, r0 + tg)
        h = _modulated_norm(x_ref[0, rows, :], g_ref[...], mod_ref[0]).astype(BF16)
        ext_ref[MAX_WINDOW + r0:MAX_WINDOW + r0 + tg, :] = _dot(h, w_ref[:, 0:D_POOL])

        pos = s * tm + r0 + lax.broadcasted_iota(jnp.int32, (tg, 1), 0)
        for gi, win in enumerate(POOL_WINDOWS):
            c0 = gi * POOL_GROUP
            cols = slice(c0, c0 + POOL_GROUP)
            acc = ext_ref[r0:r0 + tg + MAX_WINDOW, cols]
            k = 1
            while k < win:
                acc = acc + pltpu.roll(acc, k, axis=0)
                k *= 2
            u = ext_ref[MAX_WINDOW + r0:MAX_WINDOW + r0 + tg, cols]
            inv_cnt = 1.0 / jnp.minimum(pos + 1, win).astype(F32)
            p = acc[MAX_WINDOW:] * inv_cnt - u
            y = _dot(p.astype(BF16), pw_ref[gi]) * ps_ref[:, cols]
            gate = _dot(h, w_ref[:, gate0 + c0:gate0 + c0 + POOL_GROUP])
            yg_ref[0, rows, cols] = (y * _silu(gate)).astype(BF16)

        sg_ref[0, rows, :] = _silu(_dot(h, w_ref[:, gate0 + D_POOL:])).astype(BF16)
        q_ref[0, rows, :] = (_dot(h, w_ref[:, D_POOL:D_POOL + D_SB]) * 0.125).astype(BF16)
        k_ref[0, rows, :] = _dot(h, w_ref[:, D_POOL + D_SB:D_POOL + 2 * D_SB]).astype(BF16)
        v_ref[0, rows, :] = _dot(h, w_ref[:, D_POOL + 2 * D_SB:D_POOL + 3 * D_SB]).astype(BF16)


def _in0(x, mod, g, w_in, pool_w, pool_scale):
    bsz, seq, d = x.shape
    tm = TM_IN0
    n_in = w_in.shape[1]
    row_spec = pl.BlockSpec((1, tm, D_SB), lambda b, s: (b, s, 0))
    out_sds = jax.ShapeDtypeStruct((bsz, seq, D_SB), BF16)
    return pl.pallas_call(
        _in0_kernel,
        grid=(bsz, seq // tm),
        in_specs=[
            pl.BlockSpec((1, tm, d), lambda b, s: (b, s, 0)),
            pl.BlockSpec((1, 1, 3 * d), lambda b, s: (b, 0, 0)),
            pl.BlockSpec((1, d), lambda b, s: (0, 0)),
            pl.BlockSpec((d, n_in), lambda b, s: (0, 0), pipeline_mode=pl.Buffered(1)),
            pl.BlockSpec(pool_w.shape, lambda b, s: (0, 0, 0), pipeline_mode=pl.Buffered(1)),
            pl.BlockSpec((1, D_POOL), lambda b, s: (0, 0)),
        ],
        out_specs=[row_spec] * 5,
        out_shape=[out_sds] * 5,
        scratch_shapes=[pltpu.VMEM((tm + MAX_WINDOW, D_POOL), F32)],
        compiler_params=pltpu.CompilerParams(
            dimension_semantics=("arbitrary", "arbitrary"),
            vmem_limit_bytes=VMEM_LIMIT_BYTES),
        name="in0",
    )(x, mod, g, w_in, pool_w, pool_scale)


ATT_DEPTH = 5
KIND_PLAIN, KIND_DIAGONAL, KIND_DEAD = 0, 1, 2
ST_NEXT_Q, ST_IDLE, ST_SKIP = 0, 1, 2


def _softplus(z):
    return jnp.maximum(z, 0.0) + jnp.log(1.0 + jnp.exp(-jnp.abs(z)))


def _attn_kernel(q_ref, k_ref, v_ref, sg_ref, o_ref,
                 acc_ref, qm_ref, z_ref, sp_ref, ls_ref, before_ref, after_ref, a_ref, run_ref,
                 bias_ref, tri_ref, pipe_ref, pos_ref, state_ref):
    t = T_ATT
    n_blocks = q_ref.shape[1] // t
    lane = lax.broadcasted_iota(jnp.int32, (1, LANES), 1)
    head_lanes = (lane < HEAD_DIM, lane >= HEAD_DIM)
    heads = range(ATT_HEADS)

    def slab(hd):
        return slice((hd // 2) * LANES, (hd // 2 + 1) * LANES)
    row = lax.broadcasted_iota(jnp.int32, (t, t), 0)
    col = lax.broadcasted_iota(jnp.int32, (t, t), 1)
    tri_ref[...] = jnp.where(row > col, -1.0, 0.0).astype(BF16)
    causal = col < row
    bias_ref[KIND_PLAIN] = jnp.zeros((t, t), F32)
    bias_ref[KIND_DIAGONAL] = jnp.where(causal, 0.0, MASKED_LOG)
    bias_ref[KIND_DEAD] = jnp.full((t, t), MASKED_LOG, F32)
    for ref in (acc_ref, z_ref, sp_ref, before_ref, after_ref, a_ref, run_ref):
        ref[...] = jnp.zeros_like(ref)
    ls_ref[...] = jnp.full(ls_ref.shape, MASKED_LOG, BF16)
    for hd in heads:
        q2 = q_ref[0, :, slab(hd)]
        qm_ref[hd] = jnp.where(head_lanes[hd % 2], q2, jnp.zeros_like(q2))

    def rows_of(blk):
        return pl.ds(pl.multiple_of(blk * t, t), t)

    for s in range(ATT_DEPTH):
        pipe_ref[s, 0] = 0
        pipe_ref[s, 1] = 0
        pipe_ref[s, 2] = KIND_DEAD
    for w in range(2):
        pos_ref[w, 0] = 0
        pos_ref[w, 1] = -1
        state_ref[ST_SKIP + w] = 0
    state_ref[ST_NEXT_Q] = 0
    state_ref[ST_IDLE] = 0

    def advance():
        for s in range(ATT_DEPTH - 1, 0, -1):
            for c in range(3):
                pipe_ref[s, c] = pipe_ref[s - 1, c]

    def issue(walker):
        advance()
        qi, kj = pos_ref[walker, 0], pos_ref[walker, 1]
        next_q = state_ref[ST_NEXT_Q]
        fresh = jnp.logical_or(kj < 0, state_ref[ST_SKIP + walker] == 1)
        qi = jnp.where(fresh, next_q, qi)
        kj = jnp.where(fresh, next_q, kj)
        state_ref[ST_NEXT_Q] = jnp.where(fresh, jnp.minimum(next_q + 1, n_blocks), next_q)
        live = qi < n_blocks
        pipe_ref[0, 0] = jnp.where(live, qi, 0)
        pipe_ref[0, 1] = jnp.where(live, kj, 0)
        pipe_ref[0, 2] = jnp.where(live, jnp.where(kj == qi, KIND_DIAGONAL, KIND_PLAIN), KIND_DEAD)
        pos_ref[walker, 0] = qi
        pos_ref[walker, 1] = jnp.where(live, kj - 1, -1)
        state_ref[ST_IDLE] = jnp.where(live, 0, state_ref[ST_IDLE] + 1)

    def trip(par, first_stage=0):
        prev = 1 - par
        if first_stage == 0:
            issue(par)
        else:
            advance()
        qi_0, kj_0 = pipe_ref[0, 0], pipe_ref[0, 1]
        kind_1 = pipe_ref[1, 2]
        qi_4, kj_4 = pipe_ref[4, 0], pipe_ref[4, 1]

        for hd in heads:
            acc_ref[qi_4, hd] += _dot(a_ref[prev, hd], v_ref[0, rows_of(kj_4), slab(hd)])
        if first_stage <= 2:
            for hd in heads:
                after = _dot(sp_ref[prev, hd], tri_ref[...]) + before_ref[prev, hd]
                after_ref[par, hd] = after.astype(BF16)
        if first_stage <= 0:
            for hd in heads:
                z_ref[par, hd] = lax.dot_general(
                    qm_ref[hd, rows_of(qi_0), :], k_ref[0, rows_of(kj_0), slab(hd)],
                    (((1,), (1,)), ((), ())), preferred_element_type=F32)

        if first_stage <= 3:
            for hd in heads:
                a_ref[par, hd] = jnp.exp(ls_ref[par, hd] + after_ref[prev, hd])
        if first_stage > 1:
            return
        start = jnp.where(kind_1 == KIND_PLAIN, 1 + prev, 0)
        highest = None
        for hd in heads:
            z = z_ref[prev, hd] + bias_ref[kind_1]
            sp = _softplus(z)
            ls_ref[par, hd] = (z - sp).astype(BF16)
            sp_ref[par, hd] = sp.astype(BF16)
            run = run_ref[start, hd]
            before_ref[par, hd] = run
            run = run - jnp.sum(sp, axis=1, keepdims=True)
            run_ref[1 + prev, hd] = run
            top = jnp.max(run)
            highest = top if highest is None else jnp.maximum(highest, top)
        state_ref[ST_SKIP + prev] = (highest < EXP_UNDERFLOW).astype(jnp.int32)

    def two_trips(_):
        trip(0)
        trip(1)
        return state_ref[ST_IDLE]

    lax.while_loop(lambda idle: idle < 2, two_trips, jnp.int32(0))
    trip(0, first_stage=3)
    trip(1, first_stage=4)

    for i in range(n_blocks):
        rows = slice(i * t, (i + 1) * t)
        for hd in range(0, ATT_HEADS, 2):
            o = jnp.where(head_lanes[0], acc_ref[i, hd], acc_ref[i, hd + 1])
            o_ref[0, rows, slab(hd)] = (o * sg_ref[0, rows, slab(hd)].astype(F32)).astype(BF16)


def _attention(q, k, v, sg):
    bsz, seq, d = q.shape
    t = T_ATT
    nh = ATT_HEADS
    width = nh * HEAD_DIM
    full = pl.BlockSpec((1, seq, width), lambda b, hg: (b, 0, hg))
    return pl.pallas_call(
        _attn_kernel,
        grid_spec=pltpu.PrefetchScalarGridSpec(
            num_scalar_prefetch=0,
            grid=(bsz, d // width),
            in_specs=[full, full, full, full],
            out_specs=full,
            scratch_shapes=[
                pltpu.VMEM((seq // t, nh, t, LANES), F32),
                pltpu.VMEM((nh, seq, LANES), BF16),
                pltpu.VMEM((2, nh, t, t), F32),
                pltpu.VMEM((2, nh, t, t), BF16),
                pltpu.VMEM((2, nh, t, t), BF16),
                pltpu.VMEM((2, nh, t, 1), F32),
                pltpu.VMEM((2, nh, t, t), BF16),
                pltpu.VMEM((2, nh, t, t), BF16),
                pltpu.VMEM((3, nh, t, 1), F32),
                pltpu.VMEM((3, t, t), F32),
                pltpu.VMEM((t, t), BF16),
                pltpu.SMEM((ATT_DEPTH, 3), jnp.int32),
                pltpu.SMEM((2, 2), jnp.int32),
                pltpu.SMEM((ST_SKIP + 2,), jnp.int32),
            ]),
        out_shape=jax.ShapeDtypeStruct((bsz, seq, d), BF16),
        compiler_params=pltpu.CompilerParams(
            dimension_semantics=("arbitrary", "arbitrary"),
            vmem_limit_bytes=VMEM_LIMIT_BYTES),
        name="attn",
    )(q, k, v, sg)


def _out0_kernel(yp_ref, ys_ref, x_ref, mod_ref, w_ref, o_ref):
    y = _dot(yp_ref[0], w_ref[0:D_POOL, :]) + _dot(ys_ref[0], w_ref[D_POOL:, :])
    o_ref[0] = x_ref[0] + (1.0 + mod_ref[0][:, 2 * D_MODEL:]) * y


def _out0(yp, ys, x, mod, w_out):
    bsz, seq, d = x.shape
    tm = TM_OUT0
    half = pl.BlockSpec((1, tm, D_POOL), lambda b, s: (b, s, 0))
    row = pl.BlockSpec((1, tm, d), lambda b, s: (b, s, 0))
    return pl.pallas_call(
        _out0_kernel,
        grid=(bsz, seq // tm),
        in_specs=[
            half, half, row,
            pl.BlockSpec((1, 1, 3 * d), lambda b, s: (b, 0, 0)),
            pl.BlockSpec(w_out.shape, lambda b, s: (0, 0), pipeline_mode=pl.Buffered(1)),
        ],
        out_specs=row,
        out_shape=jax.ShapeDtypeStruct(x.shape, F32),
        compiler_params=pltpu.CompilerParams(
            dimension_semantics=("arbitrary", "arbitrary"),
            vmem_limit_bytes=VMEM_LIMIT_BYTES),
        name="out0",
    )(yp, ys, x, mod, w_out)


def _layer1_kernel(x_ref, mod_ref, g_ref, w_in_ref, cw_ref, cb_ref, w_out_ref, fg_ref,
                   o_ref, ext_ref, y_ref):
    s = pl.program_id(1)
    tm = x_ref.shape[1]
    pad = SUBLANES
    mod = mod_ref[0]

    @pl.when(s == 0)
    def _():
        ext_ref[0:pad, :] = jnp.zeros((pad, D_CONV), F32)

    @pl.when(s > 0)
    def _():
        ext_ref[0:pad, :] = ext_ref[tm:tm + pad, :]

    tg = tm // L1_ROW_GROUPS
    for part in range(L1_ROW_GROUPS):
        r0 = part * tg
        x = x_ref[0, r0:r0 + tg, :]
        h = _modulated_norm(x, g_ref[...], mod).astype(BF16)
        for c0 in range(0, D_CONV, CONV_CHUNK):
            cols = slice(c0, c0 + CONV_CHUNK)

            def proj(k):
                return _dot(h, w_in_ref[:, k * D_CONV + c0:k * D_CONV + c0 + CONV_CHUNK])

            u = proj(1) * proj(2)
            ext_ref[pad + r0:pad + r0 + tg, cols] = u
            conv = cb_ref[:, cols] + cw_ref[2:3, cols] * u
            for j in range(CONV_WIDTH - 1):
                shift = CONV_WIDTH - 1 - j
                conv = conv + cw_ref[j:j + 1, cols] * ext_ref[pad + r0 - shift:pad + r0 - shift + tg, cols]
            y_ref[r0:r0 + tg, cols] = (proj(0) * conv * _silu(proj(3))).astype(BF16)

        y = _dot(y_ref[r0:r0 + tg, :], w_out_ref[...])
        x2 = x + (1.0 + mod[:, 2 * D_MODEL:]) * y
        o_ref[0, r0:r0 + tg, :] = (
            x2 * lax.rsqrt(jnp.mean(x2 * x2, axis=-1, keepdims=True) + EPS) * fg_ref[...])


def _layer1(x, mod, g, w_in, conv_w, conv_b, w_out, final_g):
    bsz, seq, d = x.shape
    tm = TM_L1
    row = pl.BlockSpec((1, tm, d), lambda b, s: (b, s, 0))
    const2 = lambda b, s: (0, 0)
    return pl.pallas_call(
        _layer1_kernel,
        grid=(bsz, seq // tm),
        in_specs=[
            row,
            pl.BlockSpec((1, 1, 3 * d), lambda b, s: (b, 0, 0)),
            pl.BlockSpec((1, d), const2),
            pl.BlockSpec(w_in.shape, const2, pipeline_mode=pl.Buffered(1)),
            pl.BlockSpec(conv_w.shape, const2),
            pl.BlockSpec((1, D_CONV), const2),
            pl.BlockSpec(w_out.shape, const2, pipeline_mode=pl.Buffered(1)),
            pl.BlockSpec((1, d), const2),
        ],
        out_specs=row,
        out_shape=jax.ShapeDtypeStruct(x.shape, F32),
        scratch_shapes=[pltpu.VMEM((tm + SUBLANES, D_CONV), F32),
                        pltpu.VMEM((tm, D_CONV), BF16)],
        compiler_params=pltpu.CompilerParams(
            dimension_semantics=("arbitrary", "arbitrary"),
            vmem_limit_bytes=VMEM_LIMIT_BYTES),
        name="layer1",
    )(x, mod, g, w_in, conv_w, conv_b, w_out, final_g)


def kernel(x, c, norm_g, ada_w, ada_b, even_w_in, pool_w, pool_scale, even_w_out,
           odd_w_in, conv_w, conv_b, odd_w_out, final_g):
    bsz = x.shape[0]
    mods = _adaln(c, ada_w, ada_b)
    mod0 = mods[0].reshape(bsz, 1, 3 * D_MODEL)
    mod1 = mods[1].reshape(bsz, 1, 3 * D_MODEL)

    yg_pool, q, k, v, sg = _in0(
        x, mod0, norm_g[0:1], even_w_in[0].astype(BF16), pool_w[0].astype(BF16), pool_scale[0:1])
    yg_sb = _attention(q, k, v, sg)
    x1 = _out0(yg_pool, yg_sb, x, mod0, even_w_out[0].astype(BF16))
    return _layer1(x1, mod1, norm_g[1:2], odd_w_in[0].astype(BF16), conv_w[0], conv_b[0:1],
                   odd_w_out[0].astype(BF16), final_g.reshape(1, D_MODEL))
```

```python
import functools

import jax
import jax.numpy as jnp
from jax import lax
from jax.experimental import pallas as pl
from jax.experimental.pallas import tpu as pltpu

D_MODEL = 1024
D_INNER = 2 * D_MODEL
D_POOL = D_INNER // 2
D_SB = D_INNER - D_POOL
POOL_WINDOWS = (2, 4, 8, 16)
POOL_GROUP = D_POOL // len(POOL_WINDOWS)
MAX_WINDOW = max(POOL_WINDOWS)
HEAD_DIM = 64
CONV_WIDTH = 3
D_CONV = D_INNER
EPS = 1e-6

LANES = 128
SUBLANES = 8
VMEM_LIMIT_BYTES = 56 * 1024 * 1024

TM_IN0 = 1024
TM_OUT0 = 1024
TM_L1 = 512
T_ATT = 256
ATT_HEADS = 4
CONV_CHUNK = 512
L1_ROW_GROUPS = 2
IN0_ROW_GROUPS = 2

F32 = jnp.float32
BF16 = jnp.bfloat16

MASKED_LOG = -1e30
EXP_UNDERFLOW = -110.0


def _silu(x):
    half = 0.5 * x
    return half + half * jnp.tanh(half)


def _dot(a, b):
    return jnp.dot(a, b, preferred_element_type=F32)


def _modulated_norm(x, g, mod):
    gain = g * (1.0 + mod[:, D_MODEL:2 * D_MODEL])
    return x * lax.rsqrt(jnp.mean(x * x, axis=-1, keepdims=True) + EPS) * gain + mod[:, :D_MODEL]


def _adaln_kernel(c_ref, w_ref, b_ref, o_ref):
    o_ref[0] = _dot(_silu(c_ref[...]), w_ref[0]) + b_ref[0]


def _adaln(c, ada_w, ada_b):
    depth, d, d3 = ada_w.shape
    bsz = c.shape[0]
    tn = D_MODEL
    return pl.pallas_call(
        _adaln_kernel,
        grid=(depth, d3 // tn),
        in_specs=[
            pl.BlockSpec((bsz, d), lambda i, j: (0, 0)),
            pl.BlockSpec((1, d, tn), lambda i, j: (i, 0, j)),
            pl.BlockSpec((1, 1, tn), lambda i, j: (i, 0, j)),
        ],
        out_specs=pl.BlockSpec((1, bsz, tn), lambda i, j: (i, 0, j)),
        out_shape=jax.ShapeDtypeStruct((depth, bsz, d3), F32),
        name="adaln",
    )(c, ada_w, ada_b.reshape(depth, 1, d3))


def _in0_kernel(x_ref, mod_ref, g_ref, w_ref, pw_ref, ps_ref,
                yg_ref, q_ref, k_ref, v_ref, sg_ref, ext_ref):
    s = pl.program_id(1)
    tm = x_ref.shape[1]

    @pl.when(s == 0)
    def _():
        ext_ref[0:MAX_WINDOW, :] = jnp.zeros((MAX_WINDOW, D_POOL), F32)

    @pl.when(s > 0)
    def _():
        ext_ref[0:MAX_WINDOW, :] = ext_ref[tm:tm + MAX_WINDOW, :]

    tg = tm // IN0_ROW_GROUPS
    gate0 = D_POOL + 3 * D_SB
    for part in range(IN0_ROW_GROUPS):
        r0 = part * tg
        rows = slice(r0, r0 + tg)
        h = _modulated_norm(x_ref[0, rows, :], g_ref[...], mod_ref[0]).astype(BF16)
        ext_ref[MAX_WINDOW + r0:MAX_WINDOW + r0 + tg, :] = _dot(h, w_ref[:, 0:D_POOL])

        pos = s * tm + r0 + lax.broadcasted_iota(jnp.int32, (tg, 1), 0)
        for gi, win in enumerate(POOL_WINDOWS):
            c0 = gi * POOL_GROUP
            cols = slice(c0, c0 + POOL_GROUP)
            acc = ext_ref[r0:r0 + tg + MAX_WINDOW, cols]
            k = 1
            while k < win:
                acc = acc + pltpu.roll(acc, k, axis=0)
                k *= 2
            u = ext_ref[MAX_WINDOW + r0:MAX_WINDOW + r0 + tg, cols]
            inv_cnt = 1.0 / jnp.minimum(pos + 1, win).astype(F32)
            p = acc[MAX_WINDOW:] * inv_cnt - u
            y = _dot(p.astype(BF16), pw_ref[gi]) * ps_ref[:, cols]
            gate = _dot(h, w_ref[:, gate0 + c0:gate0 + c0 + POOL_GROUP])
            yg_ref[0, rows, cols] = (y * _silu(gate)).astype(BF16)

        sg_ref[0, rows, :] = _silu(_dot(h, w_ref[:, gate0 + D_POOL:])).astype(BF16)
        q_ref[0, rows, :] = (_dot(h, w_ref[:, D_POOL:D_POOL + D_SB]) * 0.125).astype(BF16)
        k_ref[0, rows, :] = _dot(h, w_ref[:, D_POOL + D_SB:D_POOL + 2 * D_SB]).astype(BF16)
        v_ref[0, rows, :] = _dot(h, w_ref[:, D_POOL + 2 * D_SB:D_POOL + 3 * D_SB]).astype(BF16)


def _in0(x, mod, g, w_in, pool_w, pool_scale):
    bsz, seq, d = x.shape
    tm = TM_IN0
    n_in = w_in.shape[1]
    row_spec = pl.BlockSpec((1, tm, D_SB), lambda b, s: (b, s, 0))
    out_sds = jax.ShapeDtypeStruct((bsz, seq, D_SB), BF16)
    return pl.pallas_call(
        _in0_kernel,
        grid=(bsz, seq // tm),
        in_specs=[
            pl.BlockSpec((1, tm, d), lambda b, s: (b, s, 0)),
            pl.BlockSpec((1, 1, 3 * d), lambda b, s: (b, 0, 0)),
            pl.BlockSpec((1, d), lambda b, s: (0, 0)),
            pl.BlockSpec((d, n_in), lambda b, s: (0, 0), pipeline_mode=pl.Buffered(1)),
            pl.BlockSpec(pool_w.shape, lambda b, s: (0, 0, 0), pipeline_mode=pl.Buffered(1)),
            pl.BlockSpec((1, D_POOL), lambda b, s: (0, 0)),
        ],
        out_specs=[row_spec] * 5,
        out_shape=[out_sds] * 5,
        scratch_shapes=[pltpu.VMEM((tm + MAX_WINDOW, D_POOL), F32)],
        compiler_params=pltpu.CompilerParams(
            dimension_semantics=("arbitrary", "arbitrary"),
            vmem_limit_bytes=VMEM_LIMIT_BYTES),
        name="in0",
    )(x, mod, g, w_in, pool_w, pool_scale)


ATT_DEPTH = 5
KIND_PLAIN, KIND_DIAGONAL, KIND_DEAD = 0, 1, 2
ST_NEXT_Q, ST_IDLE, ST_SKIP = 0, 1, 2


def _softplus(z):
    return jnp.maximum(z, 0.0) + jnp.log(1.0 + jnp.exp(-jnp.abs(z)))


def _attn_kernel(q_ref, k_ref, v_ref, sg_ref, o_ref,
                 acc_ref, qm_ref, z_ref, sp_ref, ls_ref, before_ref, after_ref, a_ref, run_ref,
                 bias_ref, tri_ref, pipe_ref, pos_ref, state_ref):
    t = T_ATT
    n_blocks = q_ref.shape[1] // t
    lane = lax.broadcasted_iota(jnp.int32, (1, LANES), 1)
    head_lanes = (lane < HEAD_DIM, lane >= HEAD_DIM)
    heads = range(ATT_HEADS)

    def slab(hd):
        return slice((hd // 2) * LANES, (hd // 2 + 1) * LANES)
    row = lax.broadcasted_iota(jnp.int32, (t, t), 0)
    col = lax.broadcasted_iota(jnp.int32, (t, t), 1)
    tri_ref[...] = jnp.where(row > col, -1.0, 0.0).astype(BF16)
    causal = col < row
    bias_ref[KIND_PLAIN] = jnp.zeros((t, t), F32)
    bias_ref[KIND_DIAGONAL] = jnp.where(causal, 0.0, MASKED_LOG)
    bias_ref[KIND_DEAD] = jnp.full((t, t), MASKED_LOG, F32)
    for ref in (acc_ref, z_ref, sp_ref, before_ref, after_ref, a_ref, run_ref):
        ref[...] = jnp.zeros_like(ref)
    ls_ref[...] = jnp.full(ls_ref.shape, MASKED_LOG, BF16)
    for hd in heads:
        q2 = q_ref[0, :, slab(hd)]
        qm_ref[hd] = jnp.where(head_lanes[hd % 2], q2, jnp.zeros_like(q2))

    def rows_of(blk):
        return pl.ds(pl.multiple_of(blk * t, t), t)

    for s in range(ATT_DEPTH):
        pipe_ref[s, 0] = 0
        pipe_ref[s, 1] = 0
        pipe_ref[s, 2] = KIND_DEAD
    for w in range(2):
        pos_ref[w, 0] = 0
        pos_ref[w, 1] = -1
        state_ref[ST_SKIP + w] = 0
    state_ref[ST_NEXT_Q] = 0
    state_ref[ST_IDLE] = 0

    def advance():
        for s in range(ATT_DEPTH - 1, 0, -1):
            for c in range(3):
                pipe_ref[s, c] = pipe_ref[s - 1, c]

    def issue(walker):
        advance()
        qi, kj = pos_ref[walker, 0], pos_ref[walker, 1]
        next_q = state_ref[ST_NEXT_Q]
        fresh = jnp.logical_or(kj < 0, state_ref[ST_SKIP + walker] == 1)
        qi = jnp.where(fresh, next_q, qi)
        kj = jnp.where(fresh, next_q, kj)
        state_ref[ST_NEXT_Q] = jnp.where(fresh, jnp.minimum(next_q + 1, n_blocks), next_q)
        live = qi < n_blocks
        pipe_ref[0, 0] = jnp.where(live, qi, 0)
        pipe_ref[0, 1] = jnp.where(live, kj, 0)
        pipe_ref[0, 2] = jnp.where(live, jnp.where(kj == qi, KIND_DIAGONAL, KIND_PLAIN), KIND_DEAD)
        pos_ref[walker, 0] = qi
        pos_ref[walker, 1] = jnp.where(live, kj - 1, -1)
        state_ref[ST_IDLE] = jnp.where(live, 0, state_ref[ST_IDLE] + 1)

    def trip(par, first_stage=0):
        prev = 1 - par
        if first_stage == 0:
            issue(par)
        else:
            advance()
        qi_0, kj_0 = pipe_ref[0, 0], pipe_ref[0, 1]
        kind_1 = pipe_ref[1, 2]
        qi_4, kj_4 = pipe_ref[4, 0], pipe_ref[4, 1]

        for hd in heads:
            acc_ref[qi_4, hd] += _dot(a_ref[prev, hd], v_ref[0, rows_of(kj_4), slab(hd)])
        if first_stage <= 2:
            for hd in heads:
                after = _dot(sp_ref[prev, hd], tri_ref[...]) + before_ref[prev, hd]
                after_ref[par, hd] = after.astype(BF16)
        if first_stage <= 0:
            for hd in heads:
                z_ref[par, hd] = lax.dot_general(
                    qm_ref[hd, rows_of(qi_0), :], k_ref[0, rows_of(kj_0), slab(hd)],
                    (((1,), (1,)), ((), ())), preferred_element_type=F32)

        if first_stage <= 3:
            for hd in heads:
                a_ref[par, hd] = jnp.exp(ls_ref[par, hd] + after_ref[prev, hd])
        if first_stage > 1:
            return
        start = jnp.where(kind_1 == KIND_PLAIN, 1 + prev, 0)
        highest = None
        for hd in heads:
            z = z_ref[prev, hd] + bias_ref[kind_1]
            sp = _softplus(z)
            ls_ref[par, hd] = (z - sp).astype(BF16)
            sp_ref[par, hd] = sp.astype(BF16)
            run = run_ref[start, hd]
            before_ref[par, hd] = run
            run = run - jnp.sum(sp, axis=1, keepdims=True)
            run_ref[1 + prev, hd] = run
            top = jnp.max(run)
            highest = top if highest is None else jnp.maximum(highest, top)
        state_ref[ST_SKIP + prev] = (highest < EXP_UNDERFLOW).astype(jnp.int32)

    def two_trips(_):
        trip(0)
        trip(1)
        return state_ref[ST_IDLE]

    lax.while_loop(lambda idle: idle < 2, two_trips, jnp.int32(0))
    trip(0, first_stage=3)
    trip(1, first_stage=4)

    for i in range(n_blocks):
        rows = slice(i * t, (i + 1) * t)
        for hd in range(0, ATT_HEADS, 2):
            o = jnp.where(head_lanes[0], acc_ref[i, hd], acc_ref[i, hd + 1])
            o_ref[0, rows, slab(hd)] = (o * sg_ref[0, rows, slab(hd)].astype(F32)).astype(BF16)


def _attention(q, k, v, sg):
    bsz, seq, d = q.shape
    t = T_ATT
    nh = ATT_HEADS
    width = nh * HEAD_DIM
    full = pl.BlockSpec((1, seq, width), lambda b, hg: (b, 0, hg))
    return pl.pallas_call(
        _attn_kernel,
        grid_spec=pltpu.PrefetchScalarGridSpec(
            num_scalar_prefetch=0,
            grid=(bsz, d // width),
            in_specs=[full, full, full, full],
            out_specs=full,
            scratch_shapes=[
                pltpu.VMEM((seq // t, nh, t, LANES), F32),
                pltpu.VMEM((nh, seq, LANES), BF16),
                pltpu.VMEM((2, nh, t, t), F32),
                pltpu.VMEM((2, nh, t, t), BF16),
                pltpu.VMEM((2, nh, t, t), BF16),
                pltpu.VMEM((2, nh, t, 1), F32),
                pltpu.VMEM((2, nh, t, t), BF16),
                pltpu.VMEM((2, nh, t, t), BF16),
                pltpu.VMEM((3, nh, t, 1), F32),
                pltpu.VMEM((3, t, t), F32),
                pltpu.VMEM((t, t), BF16),
                pltpu.SMEM((ATT_DEPTH, 3), jnp.int32),
                pltpu.SMEM((2, 2), jnp.int32),
                pltpu.SMEM((ST_SKIP + 2,), jnp.int32),
            ]),
        out_shape=jax.ShapeDtypeStruct((bsz, seq, d), BF16),
        compiler_params=pltpu.CompilerParams(
            dimension_semantics=("arbitrary", "arbitrary"),
            vmem_limit_bytes=VMEM_LIMIT_BYTES),
        name="attn",
    )(q, k, v, sg)


def _out0_kernel(yp_ref, ys_ref, x_ref, mod_ref, w_ref, o_ref):
    y = _dot(yp_ref[0], w_ref[0:D_POOL, :]) + _dot(ys_ref[0], w_ref[D_POOL:, :])
    o_ref[0] = x_ref[0] + (1.0 + mod_ref[0][:, 2 * D_MODEL:]) * y


def _out0(yp, ys, x, mod, w_out):
    bsz, seq, d = x.shape
    tm = TM_OUT0
    half = pl.BlockSpec((1, tm, D_POOL), lambda b, s: (b, s, 0))
    row = pl.BlockSpec((1, tm, d), lambda b, s: (b, s, 0))
    return pl.pallas_call(
        _out0_kernel,
        grid=(bsz, seq // tm),
        in_specs=[
            half, half, row,
            pl.BlockSpec((1, 1, 3 * d), lambda b, s: (b, 0, 0)),
            pl.BlockSpec(w_out.shape, lambda b, s: (0, 0), pipeline_mode=pl.Buffered(1)),
        ],
        out_specs=row,
        out_shape=jax.ShapeDtypeStruct(x.shape, F32),
        compiler_params=pltpu.CompilerParams(
            dimension_semantics=("arbitrary", "arbitrary"),
            vmem_limit_bytes=VMEM_LIMIT_BYTES),
        name="out0",
    )(yp, ys, x, mod, w_out)


def _layer1_kernel(x_ref, mod_ref, g_ref, w_in_ref, cw_ref, cb_ref, w_out_ref, fg_ref,
                   o_ref, ext_ref, y_ref):
    s = pl.program_id(1)
    tm = x_ref.shape[1]
    pad = SUBLANES
    mod = mod_ref[0]

    @pl.when(s == 0)
    def _():
        ext_ref[0:pad, :] = jnp.zeros((pad, D_CONV), F32)

    @pl.when(s > 0)
    def _():
        ext_ref[0:pad, :] = ext_ref[tm:tm + pad, :]

    tg = tm // L1_ROW_GROUPS
    for part in range(L1_ROW_GROUPS):
        r0 = part * tg
        x = x_ref[0, r0:r0 + tg, :]
        h = _modulated_norm(x, g_ref[...], mod).astype(BF16)
        for c0 in range(0, D_CONV, CONV_CHUNK):
            cols = slice(c0, c0 + CONV_CHUNK)

            def proj(k):
                return _dot(h, w_in_ref[:, k * D_CONV + c0:k * D_CONV + c0 + CONV_CHUNK])

            u = proj(1) * proj(2)
            ext_ref[pad + r0:pad + r0 + tg, cols] = u
            conv = cb_ref[:, cols] + cw_ref[2:3, cols] * u
            for j in range(CONV_WIDTH - 1):
                shift = CONV_WIDTH - 1 - j
                conv = conv + cw_ref[j:j + 1, cols] * ext_ref[pad + r0 - shift:pad + r0 - shift + tg, cols]
            y_ref[r0:r0 + tg, cols] = (proj(0) * conv * _silu(proj(3))).astype(BF16)

        y = _dot(y_ref[r0:r0 + tg, :], w_out_ref[...])
        x2 = x + (1.0 + mod[:, 2 * D_MODEL:]) * y
        o_ref[0, r0:r0 + tg, :] = (
            x2 * lax.rsqrt(jnp.mean(x2 * x2, axis=-1, keepdims=True) + EPS) * fg_ref[...])


def _layer1(x, mod, g, w_in, conv_w, conv_b, w_out, final_g):
    bsz, seq, d = x.shape
    tm = TM_L1
    row = pl.BlockSpec((1, tm, d), lambda b, s: (b, s, 0))
    const2 = lambda b, s: (0, 0)
    return pl.pallas_call(
        _layer1_kernel,
        grid=(bsz, seq // tm),
        in_specs=[
            row,
            pl.BlockSpec((1, 1, 3 * d), lambda b, s: (b, 0, 0)),
            pl.BlockSpec((1, d), const2),
            pl.BlockSpec(w_in.shape, const2, pipeline_mode=pl.Buffered(1)),
            pl.BlockSpec(conv_w.shape, const2),
            pl.BlockSpec((1, D_CONV), const2),
            pl.BlockSpec(w_out.shape, const2, pipeline_mode=pl.Buffered(1)),
            pl.BlockSpec((1, d), const2),
        ],
        out_specs=row,
        out_shape=jax.ShapeDtypeStruct(x.shape, F32),
        scratch_shapes=[pltpu.VMEM((tm + SUBLANES, D_CONV), F32),
                        pltpu.VMEM((tm, D_CONV), BF16)],
        compiler_params=pltpu.CompilerParams(
            dimension_semantics=("arbitrary", "arbitrary"),
            vmem_limit_bytes=VMEM_LIMIT_BYTES),
        name="layer1",
    )(x, mod, g, w_in, conv_w, conv_b, w_out, final_g)


def kernel(x, c, norm_g, ada_w, ada_b, even_w_in, pool_w, pool_scale, even_w_out,
           odd_w_in, conv_w, conv_b, odd_w_out, final_g):
    bsz = x.shape[0]
    mods = _adaln(c, ada_w, ada_b)
    mod0 = mods[0].reshape(bsz, 1, 3 * D_MODEL)
    mod1 = mods[1].reshape(bsz, 1, 3 * D_MODEL)

    yg_pool, q, k, v, sg = _in0(
        x, mod0, norm_g[0:1], even_w_in[0].astype(BF16), pool_w[0].astype(BF16), pool_scale[0:1])
    yg_sb = _attention(q, k, v, sg)
    x1 = _out0(yg_pool, yg_sb, x, mod0, even_w_out[0].astype(BF16))
    return _layer1(x1, mod1, norm_g[1:2], odd_w_in[0].astype(BF16), conv_w[0], conv_b[0:1],
                   odd_w_out[0].astype(BF16), final_g.reshape(1, D_MODEL))
```

```python
import functools

import jax
import jax.numpy as jnp
from jax import lax
from jax.experimental import pallas as pl
from jax.experimental.pallas import tpu as pltpu

D_MODEL = 1024
D_INNER = 2 * D_MODEL
D_POOL = D_INNER // 2
D_SB = D_INNER - D_POOL
POOL_WINDOWS = (2, 4, 8, 16)
POOL_GROUP = D_POOL // len(POOL_WINDOWS)
MAX_WINDOW = max(POOL_WINDOWS)
HEAD_DIM = 64
CONV_WIDTH = 3
D_CONV = D_INNER
EPS = 1e-6

LANES = 128
SUBLANES = 8
VMEM_LIMIT_BYTES = 56 * 1024 * 1024

TM_IN0 = 1024
TM_OUT0 = 1024
TM_L1 = 512
T_ATT = 256
ATT_HEADS = 4
CONV_CHUNK = 512
L1_ROW_GROUPS = 2
IN0_ROW_GROUPS = 2

F32 = jnp.float32
BF16 = jnp.bfloat16

MASKED_LOG = -1e30
EXP_UNDERFLOW = -110.0


def _silu(x):
    half = 0.5 * x
    return half + half * jnp.tanh(half)


def _dot(a, b):
    return jnp.dot(a, b, preferred_element_type=F32)


def _modulated_norm(x, g, mod):
    gain = g * (1.0 + mod[:, D_MODEL:2 * D_MODEL])
    return x * lax.rsqrt(jnp.mean(x * x, axis=-1, keepdims=True) + EPS) * gain + mod[:, :D_MODEL]


def _adaln_kernel(c_ref, w_ref, b_ref, o_ref):
    o_ref[0] = _dot(_silu(c_ref[...]), w_ref[0]) + b_ref[0]


def _adaln(c, ada_w, ada_b):
    depth, d, d3 = ada_w.shape
    bsz = c.shape[0]
    tn = D_MODEL
    return pl.pallas_call(
        _adaln_kernel,
        grid=(depth, d3 // tn),
        in_specs=[
            pl.BlockSpec((bsz, d), lambda i, j: (0, 0)),
            pl.BlockSpec((1, d, tn), lambda i, j: (i, 0, j)),
            pl.BlockSpec((1, 1, tn), lambda i, j: (i, 0, j)),
        ],
        out_specs=pl.BlockSpec((1, bsz, tn), lambda i, j: (i, 0, j)),
        out_shape=jax.ShapeDtypeStruct((depth, bsz, d3), F32),
        name="adaln",
    )(c, ada_w, ada_b.reshape(depth, 1, d3))


def _in0_kernel(x_ref, mod_ref, g_ref, w_ref, pw_ref, ps_ref,
                yg_ref, q_ref, k_ref, v_ref, sg_ref, ext_ref):
    s = pl.program_id(1)
    tm = x_ref.shape[1]

    @pl.when(s == 0)
    def _():
        ext_ref[0:MAX_WINDOW, :] = jnp.zeros((MAX_WINDOW, D_POOL), F32)

    @pl.when(s > 0)
    def _():
        ext_ref[0:MAX_WINDOW, :] = ext_ref[tm:tm + MAX_WINDOW, :]

    tg = tm // IN0_ROW_GROUPS
    gate0 = D_POOL + 3 * D_SB
    for part in range(IN0_ROW_GROUPS):
        r0 = part * tg
        rows = slice(r0, r0 + tg)
        h = _modulated_norm(x_ref[0, rows, :], g_ref[...], mod_ref[0]).astype(BF16)
        ext_ref[MAX_WINDOW + r0:MAX_WINDOW + r0 + tg, :] = _dot(h, w_ref[:, 0:D_POOL])

        pos = s * tm + r0 + lax.broadcasted_iota(jnp.int32, (tg, 1), 0)
        for gi, win in enumerate(POOL_WINDOWS):
            c0 = gi * POOL_GROUP
            cols = slice(c0, c0 + POOL_GROUP)
            acc = ext_ref[r0:r0 + tg + MAX_WINDOW, cols]
            k = 1
            while k < win:
                acc = acc + pltpu.roll(acc, k, axis=0)
                k *= 2
            u = ext_ref[MAX_WINDOW + r0:MAX_WINDOW + r0 + tg, cols]
            inv_cnt = 1.0 / jnp.minimum(pos + 1, win).astype(F32)
            p = acc[MAX_WINDOW:] * inv_cnt - u
            y = _dot(p.astype(BF16), pw_ref[gi]) * ps_ref[:, cols]
            gate = _dot(h, w_ref[:, gate0 + c0:gate0 + c0 + POOL_GROUP])
            yg_ref[0, rows, cols] = (y * _silu(gate)).astype(BF16)

        sg_ref[0, rows, :] = _silu(_dot(h, w_ref[:, gate0 + D_POOL:])).astype(BF16)
        q_ref[0, rows, :] = (_dot(h, w_ref[:, D_POOL:D_POOL + D_SB]) * 0.125).astype(BF16)
        k_ref[0, rows, :] = _dot(h, w_ref[:, D_POOL + D_SB:D_POOL + 2 * D_SB]).astype(BF16)
        v_ref[0, rows, :] = _dot(h, w_ref[:, D_POOL + 2 * D_SB:D_POOL + 3 * D_SB]).astype(BF16)


def _in0(x, mod, g, w_in, pool_w, pool_scale):
    bsz, seq, d = x.shape
    tm = TM_IN0
    n_in = w_in.shape[1]
    row_spec = pl.BlockSpec((1, tm, D_SB), lambda b, s: (b, s, 0))
    out_sds = jax.ShapeDtypeStruct((bsz, seq, D_SB), BF16)
    return pl.pallas_call(
        _in0_kernel,
        grid=(bsz, seq // tm),
        in_specs=[
            pl.BlockSpec((1, tm, d), lambda b, s: (b, s, 0)),
            pl.BlockSpec((1, 1, 3 * d), lambda b, s: (b, 0, 0)),
            pl.BlockSpec((1, d), lambda b, s: (0, 0)),
            pl.BlockSpec((d, n_in), lambda b, s: (0, 0), pipeline_mode=pl.Buffered(1)),
            pl.BlockSpec(pool_w.shape, lambda b, s: (0, 0, 0), pipeline_mode=pl.Buffered(1)),
            pl.BlockSpec((1, D_POOL), lambda b, s: (0, 0)),
        ],
        out_specs=[row_spec] * 5,
        out_shape=[out_sds] * 5,
        scratch_shapes=[pltpu.VMEM((tm + MAX_WINDOW, D_POOL), F32)],
        compiler_params=pltpu.CompilerParams(
            dimension_semantics=("arbitrary", "arbitrary"),
            vmem_limit_bytes=VMEM_LIMIT_BYTES),
        name="in0",
    )(x, mod, g, w_in, pool_w, pool_scale)


ATT_DEPTH = 5
KIND_PLAIN, KIND_DIAGONAL, KIND_DEAD = 0, 1, 2
ST_NEXT_Q, ST_IDLE, ST_SKIP = 0, 1, 2


def _softplus(z):
    return jnp.maximum(z, 0.0) + jnp.log(1.0 + jnp.exp(-jnp.abs(z)))


def _attn_kernel(q_ref, k_ref, v_ref, sg_ref, o_ref,
                 acc_ref, qm_ref, z_ref, sp_ref, ls_ref, before_ref, after_ref, a_ref, run_ref,
                 bias_ref, tri_ref, pipe_ref, pos_ref, state_ref):
    t = T_ATT
    n_blocks = q_ref.shape[1] // t
    lane = lax.broadcasted_iota(jnp.int32, (1, LANES), 1)
    head_lanes = (lane < HEAD_DIM, lane >= HEAD_DIM)
    heads = range(ATT_HEADS)

    def slab(hd):
        return slice((hd // 2) * LANES, (hd // 2 + 1) * LANES)
    row = lax.broadcasted_iota(jnp.int32, (t, t), 0)
    col = lax.broadcasted_iota(jnp.int32, (t, t), 1)
    tri_ref[...] = jnp.where(row > col, -1.0, 0.0).astype(BF16)
    causal = col < row
    bias_ref[KIND_PLAIN] = jnp.zeros((t, t), F32)
    bias_ref[KIND_DIAGONAL] = jnp.where(causal, 0.0, MASKED_LOG)
    bias_ref[KIND_DEAD] = jnp.full((t, t), MASKED_LOG, F32)
    acc_ref[...] = jnp.zeros_like(acc_ref)
    run_ref[0] = jnp.zeros(run_ref.shape[1:], F32)
    for ref in (z_ref, sp_ref, before_ref, after_ref, a_ref):
        ref[1] = jnp.zeros(ref.shape[1:], ref.dtype)
    ls_ref[...] = jnp.full(ls_ref.shape, MASKED_LOG, BF16)
    for hd in heads:
        q2 = q_ref[0, :, slab(hd)]
        qm_ref[hd] = jnp.where(head_lanes[hd % 2], q2, jnp.zeros_like(q2))

    def rows_of(blk):
        return pl.ds(pl.multiple_of(blk * t, t), t)

    for s in range(ATT_DEPTH):
        pipe_ref[s, 0] = 0
        pipe_ref[s, 1] = 0
        pipe_ref[s, 2] = KIND_DEAD
    for w in range(2):
        pos_ref[w, 0] = 0
        pos_ref[w, 1] = -1
        state_ref[ST_SKIP + w] = 0
    state_ref[ST_NEXT_Q] = 0
    state_ref[ST_IDLE] = 0

    def advance():
        for s in range(ATT_DEPTH - 1, 0, -1):
            for c in range(3):
                pipe_ref[s, c] = pipe_ref[s - 1, c]

    def issue(walker):
        advance()
        qi, kj = pos_ref[walker, 0], pos_ref[walker, 1]
        next_q = state_ref[ST_NEXT_Q]
        fresh = jnp.logical_or(kj < 0, state_ref[ST_SKIP + walker] == 1)
        qi = jnp.where(fresh, next_q, qi)
        kj = jnp.where(fresh, next_q, kj)
        state_ref[ST_NEXT_Q] = jnp.where(fresh, jnp.minimum(next_q + 1, n_blocks), next_q)
        live = qi < n_blocks
        pipe_ref[0, 0] = jnp.where(live, qi, 0)
        pipe_ref[0, 1] = jnp.where(live, kj, 0)
        pipe_ref[0, 2] = jnp.where(live, jnp.where(kj == qi, KIND_DIAGONAL, KIND_PLAIN), KIND_DEAD)
        pos_ref[walker, 0] = qi
        pos_ref[walker, 1] = jnp.where(live, kj - 1, -1)
        state_ref[ST_IDLE] = jnp.where(live, 0, state_ref[ST_IDLE] + 1)

    def trip(par, first_stage=0):
        prev = 1 - par
        if first_stage == 0:
            issue(par)
        else:
            advance()
        qi_0, kj_0 = pipe_ref[0, 0], pipe_ref[0, 1]
        kind_1 = pipe_ref[1, 2]
        qi_4, kj_4 = pipe_ref[4, 0], pipe_ref[4, 1]

        for hd in heads:
            acc_ref[qi_4, hd] += _dot(a_ref[prev, hd], v_ref[0, rows_of(kj_4), slab(hd)])
        if first_stage <= 2:
            for hd in heads:
                after = _dot(sp_ref[prev, hd], tri_ref[...]) + before_ref[prev, hd]
                after_ref[par, hd] = after.astype(BF16)
        if first_stage <= 0:
            for hd in heads:
                z_ref[par, hd] = lax.dot_general(
                    qm_ref[hd, rows_of(qi_0), :], k_ref[0, rows_of(kj_0), slab(hd)],
                    (((1,), (1,)), ((), ())), preferred_element_type=F32)

        if first_stage <= 3:
            for hd in heads:
                a_ref[par, hd] = jnp.exp(ls_ref[par, hd] + after_ref[prev, hd])
        if first_stage > 1:
            return
        start = jnp.where(kind_1 == KIND_PLAIN, 1 + prev, 0)
        highest = None
        for hd in heads:
            z = z_ref[prev, hd] + bias_ref[kind_1]
            sp = _softplus(z)
            ls_ref[par, hd] = (z - sp).astype(BF16)
            sp_ref[par, hd] = sp.astype(BF16)
            run = run_ref[start, hd]
            before_ref[par, hd] = run
            run = run - jnp.sum(sp, axis=1, keepdims=True)
            run_ref[1 + prev, hd] = run
            top = jnp.max(run)
            highest = top if highest is None else jnp.maximum(highest, top)
        state_ref[ST_SKIP + prev] = (highest < EXP_UNDERFLOW).astype(jnp.int32)

    def two_trips(_):
        trip(0)
        trip(1)
        return state_ref[ST_IDLE]

    lax.while_loop(lambda idle: idle < 2, two_trips, jnp.int32(0))
    trip(0, first_stage=3)
    trip(1, first_stage=4)

    for i in range(n_blocks):
        rows = slice(i * t, (i + 1) * t)
        for hd in range(0, ATT_HEADS, 2):
            o = jnp.where(head_lanes[0], acc_ref[i, hd], acc_ref[i, hd + 1])
            o_ref[0, rows, slab(hd)] = (o * sg_ref[0, rows, slab(hd)].astype(F32)).astype(BF16)


def _attention(q, k, v, sg):
    bsz, seq, d = q.shape
    t = T_ATT
    nh = ATT_HEADS
    width = nh * HEAD_DIM
    full = pl.BlockSpec((1, seq, width), lambda b, hg: (b, 0, hg))
    return pl.pallas_call(
        _attn_kernel,
        grid_spec=pltpu.PrefetchScalarGridSpec(
            num_scalar_prefetch=0,
            grid=(bsz, d // width),
            in_specs=[full, full, full, full],
            out_specs=full,
            scratch_shapes=[
                pltpu.VMEM((seq // t, nh, t, LANES), F32),
                pltpu.VMEM((nh, seq, LANES), BF16),
                pltpu.VMEM((2, nh, t, t), F32),
                pltpu.VMEM((2, nh, t, t), BF16),
                pltpu.VMEM((2, nh, t, t), BF16),
                pltpu.VMEM((2, nh, t, 1), F32),
                pltpu.VMEM((2, nh, t, t), BF16),
                pltpu.VMEM((2, nh, t, t), BF16),
                pltpu.VMEM((3, nh, t, 1), F32),
                pltpu.VMEM((3, t, t), F32),
                pltpu.VMEM((t, t), BF16),
                pltpu.SMEM((ATT_DEPTH, 3), jnp.int32),
                pltpu.SMEM((2, 2), jnp.int32),
                pltpu.SMEM((ST_SKIP + 2,), jnp.int32),
            ]),
        out_shape=jax.ShapeDtypeStruct((bsz, seq, d), BF16),
        compiler_params=pltpu.CompilerParams(
            dimension_semantics=("arbitrary", "arbitrary"),
            vmem_limit_bytes=VMEM_LIMIT_BYTES),
        name="attn",
    )(q, k, v, sg)


def _out0_kernel(yp_ref, ys_ref, x_ref, mod_ref, w_ref, o_ref):
    y = _dot(yp_ref[0], w_ref[0:D_POOL, :]) + _dot(ys_ref[0], w_ref[D_POOL:, :])
    o_ref[0] = x_ref[0] + (1.0 + mod_ref[0][:, 2 * D_MODEL:]) * y


def _out0(yp, ys, x, mod, w_out):
    bsz, seq, d = x.shape
    tm = TM_OUT0
    half = pl.BlockSpec((1, tm, D_POOL), lambda b, s: (b, s, 0))
    row = pl.BlockSpec((1, tm, d), lambda b, s: (b, s, 0))
    return pl.pallas_call(
        _out0_kernel,
        grid=(bsz, seq // tm),
        in_specs=[
            half, half, row,
            pl.BlockSpec((1, 1, 3 * d), lambda b, s: (b, 0, 0)),
            pl.BlockSpec(w_out.shape, lambda b, s: (0, 0), pipeline_mode=pl.Buffered(1)),
        ],
        out_specs=row,
        out_shape=jax.ShapeDtypeStruct(x.shape, F32),
        compiler_params=pltpu.CompilerParams(
            dimension_semantics=("arbitrary", "arbitrary"),
            vmem_limit_bytes=VMEM_LIMIT_BYTES),
        name="out0",
    )(yp, ys, x, mod, w_out)


def _layer1_kernel(x_ref, mod_ref, g_ref, w_in_ref, cw_ref, cb_ref, w_out_ref, fg_ref,
                   o_ref, ext_ref, y_ref):
    s = pl.program_id(1)
    tm = x_ref.shape[1]
    pad = SUBLANES
    mod = mod_ref[0]

    @pl.when(s == 0)
    def _():
        ext_ref[0:pad, :] = jnp.zeros((pad, D_CONV), F32)

    @pl.when(s > 0)
    def _():
        ext_ref[0:pad, :] = ext_ref[tm:tm + pad, :]

    tg = tm // L1_ROW_GROUPS
    for part in range(L1_ROW_GROUPS):
        r0 = part * tg
        x = x_ref[0, r0:r0 + tg, :]
        h = _modulated_norm(x, g_ref[...], mod).astype(BF16)
        for c0 in range(0, D_CONV, CONV_CHUNK):
            cols = slice(c0, c0 + CONV_CHUNK)

            def proj(k):
                return _dot(h, w_in_ref[:, k * D_CONV + c0:k * D_CONV + c0 + CONV_CHUNK])

            u = proj(1) * proj(2)
            ext_ref[pad + r0:pad + r0 + tg, cols] = u
            conv = cb_ref[:, cols] + cw_ref[2:3, cols] * u
            for j in range(CONV_WIDTH - 1):
                shift = CONV_WIDTH - 1 - j
                conv = conv + cw_ref[j:j + 1, cols] * ext_ref[pad + r0 - shift:pad + r0 - shift + tg, cols]
            y_ref[r0:r0 + tg, cols] = (proj(0) * conv * _silu(proj(3))).astype(BF16)

        y = _dot(y_ref[r0:r0 + tg, :], w_out_ref[...])
        x2 = x + (1.0 + mod[:, 2 * D_MODEL:]) * y
        o_ref[0, r0:r0 + tg, :] = (
            x2 * lax.rsqrt(jnp.mean(x2 * x2, axis=-1, keepdims=True) + EPS) * fg_ref[...])


def _layer1(x, mod, g, w_in, conv_w, conv_b, w_out, final_g):
    bsz, seq, d = x.shape
    tm = TM_L1
    row = pl.BlockSpec((1, tm, d), lambda b, s: (b, s, 0))
    const2 = lambda b, s: (0, 0)
    return pl.pallas_call(
        _layer1_kernel,
        grid=(bsz, seq // tm),
        in_specs=[
            row,
            pl.BlockSpec((1, 1, 3 * d), lambda b, s: (b, 0, 0)),
            pl.BlockSpec((1, d), const2),
            pl.BlockSpec(w_in.shape, const2, pipeline_mode=pl.Buffered(1)),
            pl.BlockSpec(conv_w.shape, const2),
            pl.BlockSpec((1, D_CONV), const2),
            pl.BlockSpec(w_out.shape, const2, pipeline_mode=pl.Buffered(1)),
            pl.BlockSpec((1, d), const2),
        ],
        out_specs=row,
        out_shape=jax.ShapeDtypeStruct(x.shape, F32),
        scratch_shapes=[pltpu.VMEM((tm + SUBLANES, D_CONV), F32),
                        pltpu.VMEM((tm, D_CONV), BF16)],
        compiler_params=pltpu.CompilerParams(
            dimension_semantics=("arbitrary", "arbitrary"),
            vmem_limit_bytes=VMEM_LIMIT_BYTES),
        name="layer1",
    )(x, mod, g, w_in, conv_w, conv_b, w_out, final_g)


def kernel(x, c, norm_g, ada_w, ada_b, even_w_in, pool_w, pool_scale, even_w_out,
           odd_w_in, conv_w, conv_b, odd_w_out, final_g):
    bsz = x.shape[0]
    mods = _adaln(c, ada_w, ada_b)
    mod0 = mods[0].reshape(bsz, 1, 3 * D_MODEL)
    mod1 = mods[1].reshape(bsz, 1, 3 * D_MODEL)

    yg_pool, q, k, v, sg = _in0(
        x, mod0, norm_g[0:1], even_w_in[0].astype(BF16), pool_w[0].astype(BF16), pool_scale[0:1])
    yg_sb = _attention(q, k, v, sg)
    x1 = _out0(yg_pool, yg_sb, x, mod0, even_w_out[0].astype(BF16))
    return _layer1(x1, mod1, norm_g[1:2], odd_w_in[0].astype(BF16), conv_w[0], conv_b[0:1],
                   odd_w_out[0].astype(BF16), final_g.reshape(1, D_MODEL))
```
